```python
import math
import jax
import jax.numpy as jnp
from jax import lax
import numpy as np

D_MODEL = 1024
BATCH = 2
SEQ = 8192
DEPTH = 4

GRID_W = 64
CTX_LEN = 256
HEAD_DIM = 64
NA_HEADS = 6
NA_WIN_R = 8
NA_WIN_C = 16
SG_GROUPS = 4
SG_CHUNK = 128
GQA_Q_HEADS = 6
GQA_KV_HEADS = 2
Q_BLOCK = 128
ROPE_THETA = 10000.0

D_A = NA_HEADS * HEAD_DIM
D_B = SG_GROUPS * HEAD_DIM
D_C = GQA_Q_HEADS * HEAD_DIM
D_MIX = D_A + D_B + D_C
D_KV_C = GQA_KV_HEADS * HEAD_DIM
IN_SIZES = (D_A, D_A, D_A, 2 * D_B, D_C, D_KV_C, D_KV_C)
D_IN = 3 * D_A + 2 * D_B + D_C + 2 * D_KV_C
D_FF = int(math.ceil(8 * D_MODEL / 3 / 256)) * 256
N_MOD = 6
ALPHA = (2 * DEPTH) ** 0.25
BETA = (8 * DEPTH) ** -0.25
LN_EPS = 1e-6

kernel_name = "hybrid_na_sgmlp_gqa_deepnorm_prefix"


def layer_norm(x):
    xf = x.astype(jnp.float32)
    mu = jnp.mean(xf, -1, keepdims=True)
    var = jnp.mean(jnp.square(xf - mu), -1, keepdims=True)
    return ((xf - mu) * lax.rsqrt(var + LN_EPS)).astype(x.dtype)


def rms_norm(x, g):
    xf = x.astype(jnp.float32)
    y = xf * lax.rsqrt(jnp.mean(xf * xf, -1, keepdims=True) + LN_EPS)
    return y.astype(x.dtype) * g


def modulate(x, shift, scale):
    return layer_norm(x) * (1 + scale) + shift


def split_cols(p, sizes):
    out, off = [], 0
    for s in sizes:
        out.append(p[..., off:off + s])
        off += s
    return out


def heads(t, n):
    return t.reshape(t.shape[0], t.shape[1], n, HEAD_DIM)


def axial_rope_tables(n_tokens, dtype):
    t = jnp.arange(n_tokens, dtype=jnp.int32)
    row = (t // GRID_W).astype(jnp.float32)
    col = (t % GRID_W).astype(jnp.float32)
    n_freq = HEAD_DIM // 4
    inv = 1.0 / (ROPE_THETA ** (jnp.arange(n_freq, dtype=jnp.float32) / n_freq))
    ang = jnp.stack([row[:, None] * inv, col[:, None] * inv], axis=1)
    return jnp.cos(ang).astype(dtype), jnp.sin(ang).astype(dtype)


def apply_axial_rope(x, cos, sin):
    n_freq = HEAD_DIM // 4
    xr = x.reshape(*x.shape[:-1], 2, 2, n_freq)
    x1, x2 = xr[..., 0, :], xr[..., 1, :]
    c, s = cos[:, None], sin[:, None]
    out = jnp.stack([x1 * c - x2 * s, x2 * c + x1 * s], axis=-2)
    return out.reshape(x.shape)


def attend(q, k, v):
    scale = q.shape[-1] ** -0.5
    s = jnp.einsum('bqkgd,bskd->bkgqs', q, k) * scale
    p = jax.nn.softmax(s.astype(jnp.float32), axis=-1).astype(v.dtype)
    return jnp.einsum('bkgqs,bskd->bqkgd', p, v)


def neighbourhood_attention(q, k, v, k_ctx, v_ctx, rpb):
    B, S, H, Dh = q.shape
    rows = S // GRID_W
    kr = min(NA_WIN_R, rows)
    kcw = NA_WIN_C
    scale = Dh ** -0.5
    qg = q.reshape(B, rows, GRID_W, H, Dh)
    kg = k.reshape(B, rows, GRID_W, H, Dh)
    vg = v.reshape(B, rows, GRID_W, H, Dh)
    r_idx = jnp.arange(rows, dtype=jnp.int32)
    row_start = jnp.clip(r_idx - kr // 2, 0, rows - kr)
    c_idx = jnp.arange(GRID_W, dtype=jnp.int32)
    col_start = jnp.clip(c_idx - kcw // 2, 0, GRID_W - kcw)
    col_win = col_start[:, None] + jnp.arange(kcw, dtype=jnp.int32)[None, :]
    dc = col_win - c_idx[:, None] + (NA_WIN_C - 1)
    n_loc = kr * kcw

    def one_row(args):
        q_row, rs, r = args
        k_rows = lax.dynamic_slice_in_dim(kg, rs, kr, axis=1)
        v_rows = lax.dynamic_slice_in_dim(vg, rs, kr, axis=1)
        k_win = k_rows[:, :, col_win]
        v_win = v_rows[:, :, col_win]
        s_loc = jnp.einsum('bqhd,brqchd->bhqrc', q_row, k_win) * scale
        dr = rs + jnp.arange(kr, dtype=jnp.int32) - r + (NA_WIN_R - 1)
        bias = rpb[:, dr][:, :, dc]
        s_loc = s_loc + jnp.transpose(bias, (0, 2, 1, 3))[None]
        s_ctx = jnp.einsum('bqhd,blhd->bhql', q_row, k_ctx) * scale
        s = jnp.concatenate([s_loc.reshape(B, H, GRID_W, n_loc), s_ctx], axis=-1)
        p = jax.nn.softmax(s.astype(jnp.float32), axis=-1).astype(v.dtype)
        p_loc = p[..., :n_loc].reshape(B, H, GRID_W, kr, kcw)
        p_ctx = p[..., n_loc:]
        return (jnp.einsum('bhqrc,brqchd->bqhd', p_loc, v_win)
                + jnp.einsum('bhql,blhd->bqhd', p_ctx, v_ctx))

    out = lax.map(one_row, (jnp.transpose(qg, (1, 0, 2, 3, 4)), row_start, r_idx))
    return jnp.transpose(out, (1, 0, 2, 3, 4)).reshape(B, S, H, Dh)


def gqa_latent(q, k, v, k_ctx, v_ctx):
    B, S, Hq, Dh = q.shape
    G = Hq // GQA_KV_HEADS
    nb = S // Q_BLOCK
    qb = jnp.transpose(q.reshape(B, nb, Q_BLOCK, GQA_KV_HEADS, G, Dh), (1, 0, 2, 3, 4, 5))
    k_all = jnp.concatenate([k, k_ctx], axis=1)
    v_all = jnp.concatenate([v, v_ctx], axis=1)
    out = lax.map(lambda qblk: attend(qblk, k_all, v_all), qb)
    return jnp.transpose(out, (1, 0, 2, 3, 4, 5)).reshape(B, S, Hq, Dh)


def spatial_gating(z, w_s, b_s, g_sgu):
    B, N, _ = z.shape
    u, v = z[..., :D_B], z[..., D_B:]
    v = layer_norm(v) * g_sgu
    vc = v.reshape(B, N // SG_CHUNK, SG_CHUNK, SG_GROUPS, HEAD_DIM)
    mixed = jnp.einsum('gpq,bnqgc->bnpgc', w_s, vc) + jnp.transpose(b_s)[None, None, :, :, None]
    return u * mixed.reshape(B, N, D_B)


def merge_groups(o_a, o_b, o_c, g_out):
    B, N = o_b.shape[0], o_b.shape[1]
    return jnp.concatenate([
        rms_norm(o_a.reshape(B, N, D_A), g_out[:D_A]),
        rms_norm(o_b, g_out[D_A:D_A + D_B]),
        rms_norm(o_c.reshape(B, N, D_C), g_out[D_A + D_B:]),
    ], axis=-1)


def mixing_sublayer(h, hc, w_in, rpb, w_s, b_s, g_sgu, g_q, g_k, g_out, w_o, need_ctx_out):
    B, S, _ = h.shape
    qa, ka, va, zb, qc, kc, vc = split_cols(h @ w_in, IN_SIZES)
    qa_x, ka_x, va_x, zb_x, qc_x, kc_x, vc_x = split_cols(hc @ w_in, IN_SIZES)
    cos, sin = axial_rope_tables(S, h.dtype)
    ka_x, va_x = heads(ka_x, NA_HEADS), heads(va_x, NA_HEADS)
    kc_x = rms_norm(heads(kc_x, GQA_KV_HEADS), g_k)
    vc_x = heads(vc_x, GQA_KV_HEADS)
    o_a = neighbourhood_attention(heads(qa, NA_HEADS), heads(ka, NA_HEADS), heads(va, NA_HEADS),
                                  ka_x, va_x, rpb)
    o_b = spatial_gating(jax.nn.gelu(zb), w_s, b_s, g_sgu)
    q_c = apply_axial_rope(rms_norm(heads(qc, GQA_Q_HEADS), g_q), cos, sin)
    k_c = apply_axial_rope(rms_norm(heads(kc, GQA_KV_HEADS), g_k), cos, sin)
    o_c = gqa_latent(q_c, k_c, heads(vc, GQA_KV_HEADS), kc_x, vc_x)
    y = merge_groups(o_a, o_b, o_c, g_out) @ w_o
    if not need_ctx_out:
        return y, None
    L = hc.shape[1]
    o_ax = attend(heads(qa_x, NA_HEADS)[:, :, :, None], ka_x, va_x)
    o_bx = spatial_gating(jax.nn.gelu(zb_x), w_s, b_s, g_sgu)
    q_cx = rms_norm(heads(qc_x, GQA_Q_HEADS), g_q).reshape(
        B, L, GQA_KV_HEADS, GQA_Q_HEADS // GQA_KV_HEADS, HEAD_DIM)
    o_cx = attend(q_cx, kc_x, vc_x)
    yc = merge_groups(o_ax, o_bx, o_cx, g_out) @ w_o
    return y, yc


def swiglu(h, w_in, w_out):
    a, b = h @ w_in[:, :D_FF], h @ w_in[:, D_FF:]
    return (jax.nn.silu(a) * b) @ w_out


def setup_inputs(seed: int = 0) -> dict:
    key = jax.random.key(seed)
    ks = jax.random.split(key, 24)
    n = jax.random.normal
    f = jnp.float32
    return {
        "x": n(ks[0], (BATCH, SEQ, D_MODEL), f),
        "c": n(ks[1], (BATCH, D_MODEL), f),
        "ctx": n(ks[2], (BATCH, CTX_LEN, D_MODEL), f),
        "c_ctx": n(ks[3], (D_MODEL,), f),
        "w_mod": n(ks[4], (DEPTH, D_MODEL, N_MOD * D_MODEL), f) * (0.5 * D_MODEL ** -0.5),
        "b_mod": n(ks[5], (DEPTH, N_MOD * D_MODEL), f) * 0.02,
        "w_in": n(ks[6], (DEPTH, D_MODEL, D_IN), f) * D_MODEL ** -0.5,
        "rpb": n(ks[7], (DEPTH, NA_HEADS, 2 * NA_WIN_R - 1, 2 * NA_WIN_C - 1), f) * 0.02,
        "w_s": n(ks[8], (DEPTH, SG_GROUPS, SG_CHUNK, SG_CHUNK), f) * SG_CHUNK ** -0.5,
        "b_s": n(ks[9], (DEPTH, SG_GROUPS, SG_CHUNK), f) * 0.02,
        "g_sgu": 1.0 + 0.02 * n(ks[10], (DEPTH, D_B), f),
        "g_q": 1.0 + 0.02 * n(ks[11], (DEPTH, HEAD_DIM), f),
        "g_k": 1.0 + 0.02 * n(ks[12], (DEPTH, HEAD_DIM), f),
        "g_out": 1.0 + 0.02 * n(ks[13], (DEPTH, D_MIX), f),
        "w_o": n(ks[14], (DEPTH, D_MIX, D_MODEL), f) * (D_MIX ** -0.5 * BETA),
        "ln1_g": 1.0 + 0.02 * n(ks[15], (DEPTH, D_MODEL), f),
        "ln1_b": 0.02 * n(ks[16], (DEPTH, D_MODEL), f),
        "w_ffn_in": n(ks[17], (DEPTH, D_MODEL, 2 * D_FF), f) * D_MODEL ** -0.5,
        "w_ffn_out": n(ks[18], (DEPTH, D_FF, D_MODEL), f) * (D_FF ** -0.5 * BETA),
        "ln2_g": 1.0 + 0.02 * n(ks[19], (DEPTH, D_MODEL), f),
        "ln2_b": 0.02 * n(ks[20], (DEPTH, D_MODEL), f),
    }


def reference(x, c, ctx, c_ctx, w_mod, b_mod, w_in, rpb, w_s, b_s, g_sgu, g_q, g_k, g_out, w_o,
              ln1_g, ln1_b, w_ffn_in, w_ffn_out, ln2_g, ln2_b):
    sc = jax.nn.silu(c)
    sc_ctx = jax.nn.silu(c_ctx)
    for l in range(DEPTH):
        need_ctx_out = l < DEPTH - 1
        mod = split_cols((sc @ w_mod[l] + b_mod[l])[:, None, :], (D_MODEL,) * N_MOD)
        mod_x = split_cols(sc_ctx @ w_mod[l] + b_mod[l], (D_MODEL,) * N_MOD)
        sh1, sc1, g1, sh2, sc2, g2 = mod
        sh1x, sc1x, g1x, sh2x, sc2x, g2x = mod_x
        h = modulate(x, sh1, sc1)
        hc = modulate(ctx, sh1x, sc1x)
        y, yc = mixing_sublayer(h, hc, w_in[l], rpb[l], w_s[l], b_s[l], g_sgu[l], g_q[l], g_k[l],
                                g_out[l], w_o[l], need_ctx_out)
        x = layer_norm(ALPHA * x + g1 * y) * ln1_g[l] + ln1_b[l]
        x = layer_norm(ALPHA * x + g2 * swiglu(modulate(x, sh2, sc2), w_ffn_in[l], w_ffn_out[l])) \
            * ln2_g[l] + ln2_b[l]
        if need_ctx_out:
            ctx = layer_norm(ALPHA * ctx + g1x * yc) * ln1_g[l] + ln1_b[l]
            ctx = layer_norm(ALPHA * ctx + g2x * swiglu(modulate(ctx, sh2x, sc2x), w_ffn_in[l],
                                                         w_ffn_out[l])) * ln2_g[l] + ln2_b[l]
    return x
```

```python
import functools
import math

import jax
import jax.numpy as jnp
import numpy as np
from jax import lax
from jax.experimental import pallas as pl
from jax.experimental.pallas import tpu as pltpu

F32 = jnp.float32
BF16 = jnp.bfloat16

D_MODEL = 1024
HEAD_DIM = 64
GRID_W = 64
NA_HEADS = 6
NA_WIN_R = 8
NA_WIN_C = 16
SG_GROUPS = 4
SG_CHUNK = 128
GQA_Q_HEADS = 6
GQA_KV_HEADS = 2
ROPE_THETA = 10000.0
MODEL_DEPTH = 4

D_A = NA_HEADS * HEAD_DIM
D_B = SG_GROUPS * HEAD_DIM
D_C = GQA_Q_HEADS * HEAD_DIM
D_KV_C = GQA_KV_HEADS * HEAD_DIM
D_IN = 3 * D_A + 2 * D_B + D_C + 2 * D_KV_C
D_FF = int(math.ceil(8 * D_MODEL / 3 / 256)) * 256
N_MOD = 6
ALPHA = (2 * MODEL_DEPTH) ** 0.25
LN_EPS = 1e-6
ATTN_SCALE = HEAD_DIM ** -0.5
MASK_VALUE = -1e30

OFF_QA, OFF_KA, OFF_VA = 0, D_A, 2 * D_A
OFF_ZB = 3 * D_A
OFF_QC = OFF_ZB + 2 * D_B
OFF_KC = OFF_QC + D_C
OFF_VC = OFF_KC + D_KV_C

LANES = 128
TM = 512
TQ = 256
TK = 512
NA_ROWS_PER_STEP = TQ // GRID_W
FF_CHUNK = 256
MOD_TN = 1536
VMEM_LIMIT = 56 * 1024 * 1024

GQA_HEAD_ORDER = (0, 3, 1, 4, 2, 5)


def _dot(a, b):
    return jnp.dot(a, b, preferred_element_type=F32)


def _dot_nt(a, b):
    return lax.dot_general(a, b, (((1,), (1,)), ((), ())), preferred_element_type=F32)


def _layer_norm(x):
    mu = jnp.mean(x, -1, keepdims=True)
    xc = x - mu
    var = jnp.mean(xc * xc, -1, keepdims=True)
    return xc * lax.rsqrt(var + LN_EPS)


def _rms(x):
    return x * lax.rsqrt(jnp.mean(x * x, -1, keepdims=True) + LN_EPS)


def _gelu_tanh(x):
    return x * (0.5 * (1.0 + jnp.tanh(0.7978845608028654 * (x + 0.044715 * (x * x * x)))))


def _silu(x):
    return x * (1.0 / (1.0 + jnp.exp(-x)))


def _low_half(shape):
    return lax.broadcasted_iota(jnp.int32, shape, 1) < HEAD_DIM


def _pair_rms(xg, gain):
    lo = _low_half(xg.shape)
    sq = xg * xg
    s_lo = jnp.sum(jnp.where(lo, sq, 0.0), -1, keepdims=True)
    s_hi = jnp.sum(jnp.where(lo, 0.0, sq), -1, keepdims=True)
    r = jnp.where(lo, lax.rsqrt(s_lo * (1.0 / HEAD_DIM) + LN_EPS),
                  lax.rsqrt(s_hi * (1.0 / HEAD_DIM) + LN_EPS))
    return xg * r * gain


def _rope(xn, cos_t, sin_t):
    lane = lax.broadcasted_iota(jnp.int32, xn.shape, 1)
    first = (lane % 32) < 16
    partner = jnp.where(first, pltpu.roll(xn, LANES - 16, 1), pltpu.roll(xn, 16, 1))
    return xn * cos_t + partner * sin_t


def _mod_kernel(c_ref, w_ref, b_ref, o_ref):
    sc = _silu(c_ref[...]).astype(BF16)
    o_ref[...] = _dot(sc, w_ref[...].astype(BF16)) + b_ref[...]


def _modulation(c_all, w_mod, b_mod):
    depth = w_mod.shape[0]
    n_out = w_mod.shape[2]
    return pl.pallas_call(
        _mod_kernel,
        grid=(depth, n_out // MOD_TN),
        in_specs=[
            pl.BlockSpec((8, D_MODEL), lambda l, j: (0, 0)),
            pl.BlockSpec((None, D_MODEL, MOD_TN), lambda l, j: (l, 0, j)),
            pl.BlockSpec((None, 1, MOD_TN), lambda l, j: (l, 0, j)),
        ],
        out_specs=pl.BlockSpec((None, 8, MOD_TN), lambda l, j: (l, 0, j)),
        out_shape=jax.ShapeDtypeStruct((depth, 8, n_out), F32),
        compiler_params=pltpu.CompilerParams(
            dimension_semantics=("arbitrary", "arbitrary"), vmem_limit_bytes=VMEM_LIMIT),
        name="modulation",
    )(c_all, w_mod, b_mod.reshape(depth, 1, n_out))


def _inproj_kernel(x_ref, mod_ref, w_ref, cos_ref, sin_ref, wsc_ref, bsg_ref, gsgu_ref,
                   gq_ref, gk_ref, gob_ref,
                   qa_ref, ka_ref, va_ref, ob_ref, qc_ref, kc_ref, vc_ref):
    shift = mod_ref[0, :, 0:D_MODEL]
    scale = mod_ref[0, :, D_MODEL:2 * D_MODEL]
    h = (_layer_norm(x_ref[...]) * (1.0 + scale) + shift).astype(BF16)

    def proj(off, width):
        return _dot(h, w_ref[:, off:off + width])

    qa_ref[...] = (proj(OFF_QA, D_A) * ATTN_SCALE).astype(BF16)
    ka_ref[...] = proj(OFF_KA, D_A).astype(BF16)
    va_ref[...] = proj(OFF_VA, D_A).astype(BF16)

    u = _gelu_tanh(proj(OFF_ZB, D_B))
    v = (_layer_norm(_gelu_tanh(proj(OFF_ZB + D_B, D_B))) * gsgu_ref[...]).astype(BF16)
    lane_group = lax.broadcasted_iota(jnp.int32, (SG_CHUNK, D_B), 1) // HEAD_DIM
    for c in range(TM // SG_CHUNK):
        rows = slice(c * SG_CHUNK, (c + 1) * SG_CHUNK)
        vch = v[rows]
        rhs = jnp.concatenate(
            [jnp.where(lane_group == g, vch, jnp.zeros_like(vch)) for g in range(SG_GROUPS)], 0)
        mixed = _dot(wsc_ref[...], rhs) + bsg_ref[...]
        ob_ref[rows, :] = (_rms(u[rows] * mixed) * gob_ref[...]).astype(BF16)

    cos_t = cos_ref[...]
    sin_t = sin_ref[...]
    for p in range(D_C // LANES):
        xq = proj(OFF_QC + p * LANES, LANES)
        xq = _rope(_pair_rms(xq, gq_ref[...]), cos_t, sin_t)
        qc_ref[:, p * LANES:(p + 1) * LANES] = (xq * ATTN_SCALE).astype(BF16)
    xk = _rope(_pair_rms(proj(OFF_KC, D_KV_C), gk_ref[...]), cos_t, sin_t)
    kc_ref[...] = xk.astype(BF16)
    vc_ref[...] = proj(OFF_VC, D_KV_C).astype(BF16)


def _inproj(x_all, mod_l, w_in_l, cos_t, sin_t, wsc, bsg, gsgu, gq2, gk2, gob, *, n_lat, lat_per_batch,
            n_batch, rope_lat_tiles):
    t_rows = x_all.shape[0]
    n_tiles = t_rows // TM

    def row_map(t):
        return (t, 0)

    def const2(t):
        return (0, 0)

    def mod_map(t):
        return (jnp.where(t < n_lat, t // lat_per_batch, n_batch), 0, 0)

    def rope_map(t):
        return (jnp.where(t < n_lat, t % rope_lat_tiles, rope_lat_tiles), 0)

    widths = (D_A, D_A, D_A, D_B, D_C, D_KV_C, D_KV_C)
    return pl.pallas_call(
        _inproj_kernel,
        grid=(n_tiles,),
        in_specs=[
            pl.BlockSpec((TM, D_MODEL), row_map),
            pl.BlockSpec((1, 1, N_MOD * D_MODEL), mod_map),
            pl.BlockSpec((D_MODEL, D_IN), const2),
            pl.BlockSpec((TM, LANES), rope_map),
            pl.BlockSpec((TM, LANES), rope_map),
            pl.BlockSpec((SG_CHUNK, SG_GROUPS * SG_CHUNK), const2),
            pl.BlockSpec((SG_CHUNK, D_B), const2),
            pl.BlockSpec((1, D_B), const2),
            pl.BlockSpec((1, LANES), const2),
            pl.BlockSpec((1, LANES), const2),
            pl.BlockSpec((1, D_B), const2),
        ],
        out_specs=[pl.BlockSpec((TM, w), row_map) for w in widths],
        out_shape=[jax.ShapeDtypeStruct((t_rows, w), BF16) for w in widths],
        compiler_params=pltpu.CompilerParams(
            dimension_semantics=("arbitrary",), vmem_limit_bytes=VMEM_LIMIT),
        name="inproj",
    )(x_all, mod_l, w_in_l, cos_t, sin_t, wsc, bsg, gsgu, gq2, gk2, gob)


def _split_heads(qg):
    lo = _low_half(qg.shape)
    zero = jnp.zeros_like(qg)
    return jnp.concatenate([jnp.where(lo, qg, zero), jnp.where(lo, zero, qg)], 0)


def _join_heads(o, m):
    return jnp.where(_low_half((m, LANES)), o[0:m], o[m:2 * m])


def _write_merged(o_ref, rows, outs, gout_ref):
    width = len(outs) * LANES
    ss = outs[0] * outs[0]
    for o in outs[1:]:
        ss = ss + o * o
    r = lax.rsqrt(jnp.sum(ss, -1, keepdims=True) * (1.0 / width) + LN_EPS)
    for p, o in enumerate(outs):
        cols = slice(p * LANES, (p + 1) * LANES)
        o_ref[rows, cols] = (o * r * gout_ref[:, cols]).astype(BF16)


def _na_kernel(q_ref, k_ref, v_ref, kx_ref, vx_ref, bias_ref, gout_ref, o_ref, *, grid_rows, n_row_blocks):
    t = pl.program_id(1)
    n_pairs = D_A // LANES
    win = NA_WIN_R * GRID_W

    @pl.when(t < n_row_blocks)
    def _latent():
        for i in range(NA_ROWS_PER_STEP):
            r = t * NA_ROWS_PER_STEP + i
            rs = jnp.clip(r - NA_WIN_R // 2, 0, grid_rows - NA_WIN_R)
            oidx = rs - r + (NA_WIN_R - 1)
            start = pl.multiple_of(rs * GRID_W, GRID_W)
            rows = slice(i * GRID_W, (i + 1) * GRID_W)
            outs = []
            for p in range(n_pairs):
                cols = slice(p * LANES, (p + 1) * LANES)
                lhs = _split_heads(q_ref[rows, cols])
                s_loc = _dot_nt(lhs, k_ref[pl.ds(start, win), cols]) + bias_ref[p, oidx]
                s_ctx = _dot_nt(lhs, kx_ref[:, cols])
                m = jnp.maximum(jnp.max(s_loc, -1, keepdims=True), jnp.max(s_ctx, -1, keepdims=True))
                e_loc = jnp.exp(s_loc - m)
                e_ctx = jnp.exp(s_ctx - m)
                l = jnp.sum(e_loc, -1, keepdims=True) + jnp.sum(e_ctx, -1, keepdims=True)
                o = _dot(e_loc.astype(BF16), v_ref[pl.ds(start, win), cols]) \
                    + _dot(e_ctx.astype(BF16), vx_ref[:, cols])
                outs.append(_join_heads(o * (1.0 / l), GRID_W))
            _write_merged(o_ref, rows, outs, gout_ref)

    @pl.when(t == n_row_blocks)
    def _context():
        outs = []
        for p in range(n_pairs):
            cols = slice(p * LANES, (p + 1) * LANES)
            lhs = _split_heads(q_ref[:, cols])
            s = _dot_nt(lhs, kx_ref[:, cols])
            e = jnp.exp(s - jnp.max(s, -1, keepdims=True))
            o = _dot(e.astype(BF16), vx_ref[:, cols]) * (1.0 / jnp.sum(e, -1, keepdims=True))
            outs.append(_join_heads(o, TQ))
        _write_merged(o_ref, slice(0, TQ), outs, gout_ref)


def _na(qa, ka, va, bias_l, gout_a, *, n_batch, seq, ctx_len, with_ctx):
    t_rows = qa.shape[0]
    grid_rows = seq // GRID_W
    n_row_blocks = seq // TQ
    ctx_block0 = n_batch * seq // ctx_len
    q_ctx_block0 = n_batch * seq // TQ

    def q_map(b, t):
        return (jnp.where(t < n_row_blocks, b * n_row_blocks + t, q_ctx_block0 + b), 0)

    return pl.pallas_call(
        functools.partial(_na_kernel, grid_rows=grid_rows, n_row_blocks=n_row_blocks),
        grid=(n_batch, n_row_blocks + (1 if with_ctx else 0)),
        in_specs=[
            pl.BlockSpec((TQ, D_A), q_map),
            pl.BlockSpec((seq, D_A), lambda b, t: (b, 0)),
            pl.BlockSpec((seq, D_A), lambda b, t: (b, 0)),
            pl.BlockSpec((ctx_len, D_A), lambda b, t: (ctx_block0 + b, 0)),
            pl.BlockSpec((ctx_len, D_A), lambda b, t: (ctx_block0 + b, 0)),
            pl.BlockSpec(bias_l.shape, lambda b, t: (0, 0, 0, 0)),
            pl.BlockSpec((1, D_A), lambda b, t: (0, 0)),
        ],
        out_specs=pl.BlockSpec((TQ, D_A), q_map),
        out_shape=jax.ShapeDtypeStruct((t_rows, D_A), BF16),
        compiler_params=pltpu.CompilerParams(
            dimension_semantics=("arbitrary", "arbitrary"), vmem_limit_bytes=VMEM_LIMIT),
        name="na",
    )(qa, ka, va, ka, va, bias_l, gout_a)


def _gqa_kernel(q_ref, k_ref, v_ref, kx_ref, vx_ref, gout_ref, o_ref, m_ref, l_ref, acc_ref, *,
                n_q_blocks, n_k_chunks):
    t = pl.program_id(1)
    n_groups = D_C // LANES
    m_rows = GQA_Q_HEADS * TQ

    parts = []
    for j in range(GQA_KV_HEADS):
        for g in range(n_groups):
            qg = q_ref[:, g * LANES:(g + 1) * LANES]
            lo = _low_half(qg.shape)
            keep = lo if j == 0 else jnp.logical_not(lo)
            parts.append(jnp.where(keep, qg, jnp.zeros_like(qg)))
    lhs = jnp.concatenate(parts, 0)

    s = _dot_nt(lhs, kx_ref[...])
    m0 = jnp.max(s, -1, keepdims=True)
    e = jnp.exp(s - m0)
    m_ref[...] = jnp.broadcast_to(m0, (m_rows, LANES))
    l_ref[...] = jnp.broadcast_to(jnp.sum(e, -1, keepdims=True), (m_rows, LANES))
    acc_ref[...] = _dot(e.astype(BF16), vx_ref[...])

    @pl.when(t < n_q_blocks)
    def _latent_keys():
        def body(c, carry):
            start = pl.multiple_of(c * TK, TK)
            s = _dot_nt(lhs, k_ref[pl.ds(start, TK), :])
            m_prev = m_ref[:, 0:1]
            m_new = jnp.maximum(m_prev, jnp.max(s, -1, keepdims=True))
            alpha = jnp.exp(m_prev - m_new)
            e = jnp.exp(s - m_new)
            l_new = alpha * l_ref[:, 0:1] + jnp.sum(e, -1, keepdims=True)
            acc_ref[...] = alpha * acc_ref[...] + _dot(e.astype(BF16), v_ref[pl.ds(start, TK), :])
            m_ref[...] = jnp.broadcast_to(m_new, (m_rows, LANES))
            l_ref[...] = jnp.broadcast_to(l_new, (m_rows, LANES))
            return carry

        lax.fori_loop(0, n_k_chunks, body, 0)

    o = acc_ref[...] * (1.0 / l_ref[:, 0:1])
    lo = _low_half((TQ, LANES))
    outs = [jnp.where(lo, o[g * TQ:(g + 1) * TQ], o[(n_groups + g) * TQ:(n_groups + g + 1) * TQ])
            for g in range(n_groups)]
    _write_merged(o_ref, slice(0, TQ), outs, gout_ref)


def _gqa(qc, kc, vc, gout_c, *, n_batch, seq, ctx_len, with_ctx):
    t_rows = qc.shape[0]
    n_q_blocks = seq // TQ
    ctx_block0 = n_batch * seq // ctx_len
    q_ctx_block0 = n_batch * seq // TQ
    m_rows = GQA_Q_HEADS * TQ

    def q_map(b, t):
        return (jnp.where(t < n_q_blocks, b * n_q_blocks + t, q_ctx_block0 + b), 0)

    return pl.pallas_call(
        functools.partial(_gqa_kernel, n_q_blocks=n_q_blocks, n_k_chunks=seq // TK),
        grid=(n_batch, n_q_blocks + (1 if with_ctx else 0)),
        in_specs=[
            pl.BlockSpec((TQ, D_C), q_map),
            pl.BlockSpec((seq, D_KV_C), lambda b, t: (b, 0)),
            pl.BlockSpec((seq, D_KV_C), lambda b, t: (b, 0)),
            pl.BlockSpec((ctx_len, D_KV_C), lambda b, t: (ctx_block0 + b, 0)),
            pl.BlockSpec((ctx_len, D_KV_C), lambda b, t: (ctx_block0 + b, 0)),
            pl.BlockSpec((1, D_C), lambda b, t: (0, 0)),
        ],
        out_specs=pl.BlockSpec((TQ, D_C), q_map),
        out_shape=jax.ShapeDtypeStruct((t_rows, D_C), BF16),
        scratch_shapes=[
            pltpu.VMEM((m_rows, LANES), F32),
            pltpu.VMEM((m_rows, LANES), F32),
            pltpu.VMEM((m_rows, LANES), F32),
        ],
        compiler_params=pltpu.CompilerParams(
            dimension_semantics=("arbitrary", "arbitrary"), vmem_limit_bytes=VMEM_LIMIT),
        name="gqa",
    )(qc, kc, vc, kc, vc, gout_c)


def _post_kernel(oa_ref, ob_ref, oc_ref, x_ref, mod_ref, wo_ref, wa_ref, wb_ref, wout_ref,
                 ln1g_ref, ln1b_ref, ln2g_ref, ln2b_ref, o_ref, h_ref, acc_ref):
    def mod(i):
        return mod_ref[0, :, i * D_MODEL:(i + 1) * D_MODEL]

    y = (_dot(oa_ref[...], wo_ref[0:D_A, :]) + _dot(ob_ref[...], wo_ref[D_A:D_A + D_B, :])
         + _dot(oc_ref[...], wo_ref[D_A + D_B:, :]))
    x1 = _layer_norm(ALPHA * x_ref[...] + mod(2) * y) * ln1g_ref[...] + ln1b_ref[...]
    h_ref[...] = (_layer_norm(x1) * (1.0 + mod(4)) + mod(3)).astype(BF16)
    acc_ref[...] = jnp.zeros_like(acc_ref)

    def body(c, carry):
        h = h_ref[...]
        g = _silu(_dot(h, wa_ref[c])) * _dot(h, wb_ref[c])
        acc_ref[...] += _dot(g.astype(BF16), wout_ref[c])
        return carry

    lax.fori_loop(0, D_FF // FF_CHUNK, body, 0)
    o_ref[...] = _layer_norm(ALPHA * x1 + mod(5) * acc_ref[...]) * ln2g_ref[...] + ln2b_ref[...]


def _post(oa, ob, oc, x_all, mod_l, wo, wa, wb, wout, ln1g, ln1b, ln2g, ln2b, *, n_lat, lat_per_batch,
          n_batch, n_tiles):
    t_rows = x_all.shape[0]
    n_ff = D_FF // FF_CHUNK

    def row_map(t):
        return (t, 0)

    def const2(t):
        return (0, 0)

    def const3(t):
        return (0, 0, 0)

    def mod_map(t):
        return (jnp.where(t < n_lat, t // lat_per_batch, n_batch), 0, 0)

    return pl.pallas_call(
        _post_kernel,
        grid=(n_tiles,),
        in_specs=[
            pl.BlockSpec((TM, D_A), row_map),
            pl.BlockSpec((TM, D_B), row_map),
            pl.BlockSpec((TM, D_C), row_map),
            pl.BlockSpec((TM, D_MODEL), row_map),
            pl.BlockSpec((1, 1, N_MOD * D_MODEL), mod_map),
            pl.BlockSpec((D_MODEL, D_MODEL), const2),
            pl.BlockSpec((n_ff, D_MODEL, FF_CHUNK), const3),
            pl.BlockSpec((n_ff, D_MODEL, FF_CHUNK), const3),
            pl.BlockSpec((n_ff, FF_CHUNK, D_MODEL), const3),
            pl.BlockSpec((1, D_MODEL), const2),
            pl.BlockSpec((1, D_MODEL), const2),
            pl.BlockSpec((1, D_MODEL), const2),
            pl.BlockSpec((1, D_MODEL), const2),
        ],
        out_specs=pl.BlockSpec((TM, D_MODEL), row_map),
        out_shape=jax.ShapeDtypeStruct((t_rows, D_MODEL), F32),
        scratch_shapes=[pltpu.VMEM((TM, D_MODEL), BF16), pltpu.VMEM((TM, D_MODEL), F32)],
        compiler_params=pltpu.CompilerParams(
            dimension_semantics=("arbitrary",), vmem_limit_bytes=VMEM_LIMIT),
        name="post",
    )(oa, ob, oc, x_all, mod_l, wo, wa, wb, wout, ln1g, ln1b, ln2g, ln2b)


def _rope_tables(seq):
    t = jnp.arange(seq, dtype=jnp.int32)
    row = (t // GRID_W).astype(F32)
    col = (t % GRID_W).astype(F32)
    n_freq = HEAD_DIM // 4
    inv = 1.0 / (ROPE_THETA ** (jnp.arange(n_freq, dtype=F32) / n_freq))
    ang = jnp.stack([row[:, None] * inv, col[:, None] * inv], axis=1)
    cos = jnp.cos(ang)[:, :, None, :]
    sin = jnp.sin(ang)[:, :, None, :]
    cos_h = jnp.broadcast_to(cos, (seq, 2, 2, n_freq)).reshape(seq, HEAD_DIM)
    sign = jnp.array([-1.0, 1.0], F32)[None, None, :, None]
    sin_h = jnp.broadcast_to(sin * sign, (seq, 2, 2, n_freq)).reshape(seq, HEAD_DIM)
    cos_t = jnp.concatenate([jnp.tile(cos_h, (1, 2)), jnp.ones((TM, LANES), F32)], 0)
    sin_t = jnp.concatenate([jnp.tile(sin_h, (1, 2)), jnp.zeros((TM, LANES), F32)], 0)
    return cos_t, sin_t


def _na_bias_tables(rpb):
    depth = rpb.shape[0]
    c_idx = np.arange(GRID_W)
    col_start = np.clip(c_idx - NA_WIN_C // 2, 0, GRID_W - NA_WIN_C)
    kc = np.arange(GRID_W)
    valid = (kc[None, :] >= col_start[:, None]) & (kc[None, :] < col_start[:, None] + NA_WIN_C)
    dc = np.clip(kc[None, :] - c_idx[:, None] + (NA_WIN_C - 1), 0, 2 * NA_WIN_C - 2)
    dr = np.arange(NA_WIN_R)[:, None] + np.arange(NA_WIN_R)[None, :]
    tab = rpb[:, :, dr][:, :, :, :, dc]
    tab = jnp.where(jnp.asarray(valid)[None, None, None, None], tab, MASK_VALUE)
    tab = jnp.transpose(tab, (0, 1, 2, 4, 3, 5))
    tab = tab.reshape(depth, NA_HEADS // 2, 2, NA_WIN_R, GRID_W, NA_WIN_R * GRID_W)
    tab = jnp.transpose(tab, (0, 1, 3, 2, 4, 5))
    return tab.reshape(depth, NA_HEADS // 2, NA_WIN_R, 2 * GRID_W, NA_WIN_R * GRID_W)


def kernel(x, c, ctx, c_ctx, w_mod, b_mod, w_in, rpb, w_s, b_s, g_sgu, g_q, g_k, g_out, w_o,
           ln1_g, ln1_b, w_ffn_in, w_ffn_out, ln2_g, ln2_b):
    n_batch, seq, _ = x.shape
    ctx_len = ctx.shape[1]
    depth = w_in.shape[0]
    assert seq % TM == 0 and (n_batch * ctx_len) % TM == 0 and ctx_len == TQ and seq % TK == 0
    assert n_batch + 1 <= 8
    n_lat = n_batch * seq // TM
    n_tiles = n_lat + n_batch * ctx_len // TM
    lat_per_batch = seq // TM

    x_all = jnp.concatenate([x.reshape(n_batch * seq, D_MODEL), ctx.reshape(n_batch * ctx_len, D_MODEL)], 0)

    c_all = jnp.zeros((8, D_MODEL), F32).at[:n_batch].set(c).at[n_batch].set(c_ctx)
    mod = _modulation(c_all, w_mod, b_mod)[:, :n_batch + 1].reshape(depth, n_batch + 1, 1, N_MOD * D_MODEL)

    perm_c = np.concatenate([np.arange(h * HEAD_DIM, (h + 1) * HEAD_DIM) for h in GQA_HEAD_ORDER])
    in_cols = np.concatenate([np.arange(OFF_QC), OFF_QC + perm_c, np.arange(OFF_KC, D_IN)])
    w_in_b = w_in[:, :, in_cols].astype(BF16)
    mix_rows = np.concatenate([np.arange(D_A + D_B), D_A + D_B + perm_c])
    w_o_b = w_o[:, mix_rows, :].astype(BF16)
    g_out_p = g_out[:, mix_rows]
    n_ff = D_FF // FF_CHUNK
    w_a = jnp.transpose(w_ffn_in[:, :, :D_FF].reshape(depth, D_MODEL, n_ff, FF_CHUNK), (0, 2, 1, 3)).astype(BF16)
    w_b = jnp.transpose(w_ffn_in[:, :, D_FF:].reshape(depth, D_MODEL, n_ff, FF_CHUNK), (0, 2, 1, 3)).astype(BF16)
    w_out = w_ffn_out.reshape(depth, n_ff, FF_CHUNK, D_MODEL).astype(BF16)
    w_s_cat = jnp.transpose(w_s, (0, 2, 1, 3)).reshape(depth, SG_CHUNK, SG_GROUPS * SG_CHUNK).astype(BF16)
    b_s_full = jnp.repeat(jnp.transpose(b_s, (0, 2, 1)), HEAD_DIM, axis=2)
    g_q2 = jnp.tile(g_q, (1, 2)).reshape(depth, 1, LANES)
    g_k2 = jnp.tile(g_k, (1, 2)).reshape(depth, 1, LANES)
    cos_t, sin_t = _rope_tables(seq)
    bias = _na_bias_tables(rpb)

    def row(v):
        return v.reshape(1, -1)

    for l in range(depth):
        with_ctx = l < depth - 1
        qa, ka, va, ob, qc, kc, vc = _inproj(
            x_all, mod[l], w_in_b[l], cos_t, sin_t, w_s_cat[l], b_s_full[l], row(g_sgu[l]), g_q2[l], g_k2[l],
            row(g_out_p[l, D_A:D_A + D_B]),
            n_lat=n_lat, lat_per_batch=lat_per_batch, n_batch=n_batch, rope_lat_tiles=seq // TM)
        oa = _na(qa, ka, va, bias[l], row(g_out_p[l, :D_A]),
                 n_batch=n_batch, seq=seq, ctx_len=ctx_len, with_ctx=with_ctx)
        oc = _gqa(qc, kc, vc, row(g_out_p[l, D_A + D_B:]),
                  n_batch=n_batch, seq=seq, ctx_len=ctx_len, with_ctx=with_ctx)
        x_all = _post(oa, ob, oc, x_all, mod[l], w_o_b[l], w_a[l], w_b[l], w_out[l],
                      row(ln1_g[l]), row(ln1_b[l]), row(ln2_g[l]), row(ln2_b[l]),
                      n_lat=n_lat, lat_per_batch=lat_per_batch, n_batch=n_batch,
                      n_tiles=n_tiles if with_ctx else n_lat)
    return x_all[:n_batch * seq].reshape(n_batch, seq, D_MODEL)
```

```python
import functools
import math

import jax
import jax.numpy as jnp
import numpy as np
from jax import lax
from jax.experimental import pallas as pl
from jax.experimental.pallas import tpu as pltpu

F32 = jnp.float32
BF16 = jnp.bfloat16

D_MODEL = 1024
HEAD_DIM = 64
GRID_W = 64
NA_HEADS = 6
NA_WIN_R = 8
NA_WIN_C = 16
SG_GROUPS = 4
SG_CHUNK = 128
GQA_Q_HEADS = 6
GQA_KV_HEADS = 2
ROPE_THETA = 10000.0
MODEL_DEPTH = 4

D_A = NA_HEADS * HEAD_DIM
D_B = SG_GROUPS * HEAD_DIM
D_C = GQA_Q_HEADS * HEAD_DIM
D_KV_C = GQA_KV_HEADS * HEAD_DIM
D_IN = 3 * D_A + 2 * D_B + D_C + 2 * D_KV_C
D_FF = int(math.ceil(8 * D_MODEL / 3 / 256)) * 256
N_MOD = 6
ALPHA = (2 * MODEL_DEPTH) ** 0.25
LN_EPS = 1e-6
ATTN_SCALE = HEAD_DIM ** -0.5
MASK_VALUE = -1e30
LOG2_E = 1.4426950408889634
SAFE_SCORE_BOUND = 57.0

OFF_QA, OFF_KA, OFF_VA = 0, D_A, 2 * D_A
OFF_ZB = 3 * D_A
OFF_QC = OFF_ZB + 2 * D_B
OFF_KC = OFF_QC + D_C
OFF_VC = OFF_KC + D_KV_C

LANES = 128
TM = 512
TQ = 256
TK = 512
NA_ROWS_PER_STEP = TQ // GRID_W
FF_CHUNK = 256
MOD_TN = 1536
VMEM_LIMIT = 56 * 1024 * 1024

GQA_HEAD_ORDER = (0, 3, 1, 4, 2, 5)


def _dot(a, b):
    return jnp.dot(a, b, preferred_element_type=F32)


def _dot_nt(a, b):
    return lax.dot_general(a, b, (((1,), (1,)), ((), ())), preferred_element_type=F32)


def _layer_norm(x):
    mu = jnp.mean(x, -1, keepdims=True)
    xc = x - mu
    var = jnp.mean(xc * xc, -1, keepdims=True)
    return xc * lax.rsqrt(var + LN_EPS)


def _rms(x):
    return x * lax.rsqrt(jnp.mean(x * x, -1, keepdims=True) + LN_EPS)


def _gelu_tanh(x):
    return x * (0.5 * (1.0 + jnp.tanh(0.7978845608028654 * (x + 0.044715 * (x * x * x)))))


def _silu(x):
    return x * (1.0 / (1.0 + jnp.exp(-x)))


def _low_half(shape):
    return lax.broadcasted_iota(jnp.int32, shape, 1) < HEAD_DIM


def _pair_rms(xg, gain):
    lo = _low_half(xg.shape)
    sq = xg * xg
    s_lo = jnp.sum(jnp.where(lo, sq, 0.0), -1, keepdims=True)
    s_hi = jnp.sum(jnp.where(lo, 0.0, sq), -1, keepdims=True)
    r = jnp.where(lo, lax.rsqrt(s_lo * (1.0 / HEAD_DIM) + LN_EPS),
                  lax.rsqrt(s_hi * (1.0 / HEAD_DIM) + LN_EPS))
    return xg * r * gain


def _rope(xn, cos_t, sin_t):
    lane = lax.broadcasted_iota(jnp.int32, xn.shape, 1)
    first = (lane % 32) < 16
    partner = jnp.where(first, pltpu.roll(xn, LANES - 16, 1), pltpu.roll(xn, 16, 1))
    return xn * cos_t + partner * sin_t


def _mod_kernel(c_ref, w_ref, b_ref, o_ref):
    sc = _silu(c_ref[...]).astype(BF16)
    o_ref[...] = _dot(sc, w_ref[...].astype(BF16)) + b_ref[...]


def _modulation(c_all, w_mod, b_mod):
    depth = w_mod.shape[0]
    n_out = w_mod.shape[2]
    return pl.pallas_call(
        _mod_kernel,
        grid=(depth, n_out // MOD_TN),
        in_specs=[
            pl.BlockSpec((8, D_MODEL), lambda l, j: (0, 0)),
            pl.BlockSpec((None, D_MODEL, MOD_TN), lambda l, j: (l, 0, j)),
            pl.BlockSpec((None, 1, MOD_TN), lambda l, j: (l, 0, j)),
        ],
        out_specs=pl.BlockSpec((None, 8, MOD_TN), lambda l, j: (l, 0, j)),
        out_shape=jax.ShapeDtypeStruct((depth, 8, n_out), F32),
        compiler_params=pltpu.CompilerParams(
            dimension_semantics=("arbitrary", "arbitrary"), vmem_limit_bytes=VMEM_LIMIT),
        name="modulation",
    )(c_all, w_mod, b_mod.reshape(depth, 1, n_out))


def _inproj_kernel(x_ref, mod_ref, w_ref, cos_ref, sin_ref, wsc_ref, bsg_ref, gsgu_ref,
                   gq_ref, gk_ref, gob_ref,
                   qa_ref, ka_ref, va_ref, ob_ref, qc_ref, kc_ref, vc_ref):
    shift = mod_ref[0, :, 0:D_MODEL]
    scale = mod_ref[0, :, D_MODEL:2 * D_MODEL]
    h = (_layer_norm(x_ref[...]) * (1.0 + scale) + shift).astype(BF16)

    def proj(off, width):
        return _dot(h, w_ref[:, off:off + width])

    qa_ref[...] = (proj(OFF_QA, D_A) * ATTN_SCALE).astype(BF16)
    ka_ref[...] = proj(OFF_KA, D_A).astype(BF16)
    va_ref[...] = proj(OFF_VA, D_A).astype(BF16)

    u = _gelu_tanh(proj(OFF_ZB, D_B))
    v = (_layer_norm(_gelu_tanh(proj(OFF_ZB + D_B, D_B))) * gsgu_ref[...]).astype(BF16)
    lane_group = lax.broadcasted_iota(jnp.int32, (SG_CHUNK, D_B), 1) // HEAD_DIM
    for c in range(TM // SG_CHUNK):
        rows = slice(c * SG_CHUNK, (c + 1) * SG_CHUNK)
        vch = v[rows]
        rhs = jnp.concatenate(
            [jnp.where(lane_group == g, vch, jnp.zeros_like(vch)) for g in range(SG_GROUPS)], 0)
        mixed = _dot(wsc_ref[...], rhs) + bsg_ref[...]
        ob_ref[rows, :] = (_rms(u[rows] * mixed) * gob_ref[...]).astype(BF16)

    cos_t = cos_ref[...]
    sin_t = sin_ref[...]
    for p in range(D_C // LANES):
        xq = proj(OFF_QC + p * LANES, LANES)
        xq = _rope(_pair_rms(xq, gq_ref[...]), cos_t, sin_t)
        qc_ref[:, p * LANES:(p + 1) * LANES] = (xq * (ATTN_SCALE * LOG2_E)).astype(BF16)
    xk = _rope(_pair_rms(proj(OFF_KC, D_KV_C), gk_ref[...]), cos_t, sin_t)
    kc_ref[...] = xk.astype(BF16)
    vc_ref[...] = proj(OFF_VC, D_KV_C).astype(BF16)


def _inproj(x_all, mod_l, w_in_l, cos_t, sin_t, wsc, bsg, gsgu, gq2, gk2, gob, *, n_lat, lat_per_batch,
            n_batch, rope_lat_tiles):
    t_rows = x_all.shape[0]
    n_tiles = t_rows // TM

    def row_map(t):
        return (t, 0)

    def const2(t):
        return (0, 0)

    def mod_map(t):
        return (jnp.where(t < n_lat, t // lat_per_batch, n_batch), 0, 0)

    def rope_map(t):
        return (jnp.where(t < n_lat, t % rope_lat_tiles, rope_lat_tiles), 0)

    widths = (D_A, D_A, D_A, D_B, D_C, D_KV_C, D_KV_C)
    return pl.pallas_call(
        _inproj_kernel,
        grid=(n_tiles,),
        in_specs=[
            pl.BlockSpec((TM, D_MODEL), row_map),
            pl.BlockSpec((1, 1, N_MOD * D_MODEL), mod_map),
            pl.BlockSpec((D_MODEL, D_IN), const2),
            pl.BlockSpec((TM, LANES), rope_map),
            pl.BlockSpec((TM, LANES), rope_map),
            pl.BlockSpec((SG_CHUNK, SG_GROUPS * SG_CHUNK), const2),
            pl.BlockSpec((SG_CHUNK, D_B), const2),
            pl.BlockSpec((1, D_B), const2),
            pl.BlockSpec((1, LANES), const2),
            pl.BlockSpec((1, LANES), const2),
            pl.BlockSpec((1, D_B), const2),
        ],
        out_specs=[pl.BlockSpec((TM, w), row_map) for w in widths],
        out_shape=[jax.ShapeDtypeStruct((t_rows, w), BF16) for w in widths],
        compiler_params=pltpu.CompilerParams(
            dimension_semantics=("arbitrary",), vmem_limit_bytes=VMEM_LIMIT),
        name="inproj",
    )(x_all, mod_l, w_in_l, cos_t, sin_t, wsc, bsg, gsgu, gq2, gk2, gob)


def _split_heads(qg):
    lo = _low_half(qg.shape)
    zero = jnp.zeros_like(qg)
    return jnp.concatenate([jnp.where(lo, qg, zero), jnp.where(lo, zero, qg)], 0)


def _join_heads(o, m):
    return jnp.where(_low_half((m, LANES)), o[0:m], o[m:2 * m])


def _write_merged(o_ref, rows, outs, gout_ref):
    width = len(outs) * LANES
    ss = outs[0] * outs[0]
    for o in outs[1:]:
        ss = ss + o * o
    r = lax.rsqrt(jnp.sum(ss, -1, keepdims=True) * (1.0 / width) + LN_EPS)
    for p, o in enumerate(outs):
        cols = slice(p * LANES, (p + 1) * LANES)
        o_ref[rows, cols] = (o * r * gout_ref[:, cols]).astype(BF16)


def _na_kernel(q_ref, k_ref, v_ref, kx_ref, vx_ref, bias_ref, gout_ref, o_ref, *, grid_rows, n_row_blocks):
    t = pl.program_id(1)
    n_pairs = D_A // LANES
    win = NA_WIN_R * GRID_W

    @pl.when(t < n_row_blocks)
    def _latent():
        for i in range(NA_ROWS_PER_STEP):
            r = t * NA_ROWS_PER_STEP + i
            rs = jnp.clip(r - NA_WIN_R // 2, 0, grid_rows - NA_WIN_R)
            oidx = rs - r + (NA_WIN_R - 1)
            start = pl.multiple_of(rs * GRID_W, GRID_W)
            rows = slice(i * GRID_W, (i + 1) * GRID_W)
            outs = []
            for p in range(n_pairs):
                cols = slice(p * LANES, (p + 1) * LANES)
                lhs = _split_heads(q_ref[rows, cols])
                s_loc = _dot_nt(lhs, k_ref[pl.ds(start, win), cols]) + bias_ref[p, oidx]
                s_ctx = _dot_nt(lhs, kx_ref[:, cols])
                m = jnp.maximum(jnp.max(s_loc, -1, keepdims=True), jnp.max(s_ctx, -1, keepdims=True))
                e_loc = jnp.exp(s_loc - m)
                e_ctx = jnp.exp(s_ctx - m)
                l = jnp.sum(e_loc, -1, keepdims=True) + jnp.sum(e_ctx, -1, keepdims=True)
                o = _dot(e_loc.astype(BF16), v_ref[pl.ds(start, win), cols]) \
                    + _dot(e_ctx.astype(BF16), vx_ref[:, cols])
                outs.append(_join_heads(o * (1.0 / l), GRID_W))
            _write_merged(o_ref, rows, outs, gout_ref)

    @pl.when(t == n_row_blocks)
    def _context():
        outs = []
        for p in range(n_pairs):
            cols = slice(p * LANES, (p + 1) * LANES)
            lhs = _split_heads(q_ref[:, cols])
            s = _dot_nt(lhs, kx_ref[:, cols])
            e = jnp.exp(s - jnp.max(s, -1, keepdims=True))
            o = _dot(e.astype(BF16), vx_ref[:, cols]) * (1.0 / jnp.sum(e, -1, keepdims=True))
            outs.append(_join_heads(o, TQ))
        _write_merged(o_ref, slice(0, TQ), outs, gout_ref)


def _na(qa, ka, va, bias_l, gout_a, *, n_batch, seq, ctx_len, with_ctx):
    t_rows = qa.shape[0]
    grid_rows = seq // GRID_W
    n_row_blocks = seq // TQ
    ctx_block0 = n_batch * seq // ctx_len
    q_ctx_block0 = n_batch * seq // TQ

    def q_map(b, t):
        return (jnp.where(t < n_row_blocks, b * n_row_blocks + t, q_ctx_block0 + b), 0)

    return pl.pallas_call(
        functools.partial(_na_kernel, grid_rows=grid_rows, n_row_blocks=n_row_blocks),
        grid=(n_batch, n_row_blocks + (1 if with_ctx else 0)),
        in_specs=[
            pl.BlockSpec((TQ, D_A), q_map),
            pl.BlockSpec((seq, D_A), lambda b, t: (b, 0)),
            pl.BlockSpec((seq, D_A), lambda b, t: (b, 0)),
            pl.BlockSpec((ctx_len, D_A), lambda b, t: (ctx_block0 + b, 0)),
            pl.BlockSpec((ctx_len, D_A), lambda b, t: (ctx_block0 + b, 0)),
            pl.BlockSpec(bias_l.shape, lambda b, t: (0, 0, 0, 0)),
            pl.BlockSpec((1, D_A), lambda b, t: (0, 0)),
        ],
        out_specs=pl.BlockSpec((TQ, D_A), q_map),
        out_shape=jax.ShapeDtypeStruct((t_rows, D_A), BF16),
        compiler_params=pltpu.CompilerParams(
            dimension_semantics=("arbitrary", "arbitrary"), vmem_limit_bytes=VMEM_LIMIT),
        name="na",
    )(qa, ka, va, ka, va, bias_l, gout_a)


def _gqa_kernel(q_ref, k_ref, v_ref, kx_ref, vx_ref, gq_ref, gk_ref, gout_ref, o_ref,
                lhs_ref, m_ref, l_ref, acc_ref, *, n_q_blocks, n_k_chunks):
    t = pl.program_id(1)
    n_groups = D_C // LANES
    m_rows = GQA_Q_HEADS * TQ

    for j in range(GQA_KV_HEADS):
        for g in range(n_groups):
            qg = q_ref[:, g * LANES:(g + 1) * LANES]
            lo = _low_half(qg.shape)
            keep = lo if j == 0 else jnp.logical_not(lo)
            lhs_ref[(j * n_groups + g) * TQ:(j * n_groups + g + 1) * TQ, :] = jnp.where(keep, qg, jnp.zeros_like(qg))

    score_bound = (HEAD_DIM * ATTN_SCALE * LOG2_E) * jnp.max(jnp.abs(gq_ref[...])) * jnp.max(jnp.abs(gk_ref[...]))
    bounded = score_bound <= SAFE_SCORE_BOUND

    @pl.when(bounded)
    def _fixed_shift():
        def chunk(k_c, v_c):
            p = jnp.exp2(_dot_nt(lhs_ref[...], k_c) - score_bound)
            l_part = p[:, 0:LANES]
            for i in range(1, k_c.shape[0] // LANES):
                l_part = l_part + p[:, i * LANES:(i + 1) * LANES]
            return l_part, _dot(p.astype(BF16), v_c)

        l0, a0 = chunk(kx_ref[...], vx_ref[...])
        l_ref[...] = l0
        acc_ref[...] = a0

        @pl.when(t < n_q_blocks)
        def _latent_keys():
            def body(c, carry):
                start = pl.multiple_of(c * TK, TK)
                l_c, a_c = chunk(k_ref[pl.ds(start, TK), :], v_ref[pl.ds(start, TK), :])
                l_ref[...] += l_c
                acc_ref[...] += a_c
                return carry

            lax.fori_loop(0, n_k_chunks, body, 0)

        acc_ref[...] = acc_ref[...] * (1.0 / jnp.sum(l_ref[...], -1, keepdims=True))

    @pl.when(jnp.logical_not(bounded))
    def _running_max():
        s = _dot_nt(lhs_ref[...], kx_ref[...])
        m0 = jnp.max(s, -1, keepdims=True)
        e = jnp.exp2(s - m0)
        m_ref[...] = jnp.broadcast_to(m0, (m_rows, LANES))
        l_ref[...] = jnp.broadcast_to(jnp.sum(e, -1, keepdims=True), (m_rows, LANES))
        acc_ref[...] = _dot(e.astype(BF16), vx_ref[...])

        @pl.when(t < n_q_blocks)
        def _latent_keys():
            def body(c, carry):
                start = pl.multiple_of(c * TK, TK)
                s = _dot_nt(lhs_ref[...], k_ref[pl.ds(start, TK), :])
                m_prev = m_ref[:, 0:1]
                m_new = jnp.maximum(m_prev, jnp.max(s, -1, keepdims=True))
                alpha = jnp.exp2(m_prev - m_new)
                e = jnp.exp2(s - m_new)
                l_new = alpha * l_ref[:, 0:1] + jnp.sum(e, -1, keepdims=True)
                acc_ref[...] = alpha * acc_ref[...] + _dot(e.astype(BF16), v_ref[pl.ds(start, TK), :])
                m_ref[...] = jnp.broadcast_to(m_new, (m_rows, LANES))
                l_ref[...] = jnp.broadcast_to(l_new, (m_rows, LANES))
                return carry

            lax.fori_loop(0, n_k_chunks, body, 0)

        acc_ref[...] = acc_ref[...] * (1.0 / l_ref[:, 0:1])

    lo = _low_half((TQ, LANES))
    outs = [jnp.where(lo, acc_ref[g * TQ:(g + 1) * TQ, :], acc_ref[(n_groups + g) * TQ:(n_groups + g + 1) * TQ, :])
            for g in range(n_groups)]
    _write_merged(o_ref, slice(0, TQ), outs, gout_ref)


def _gqa(qc, kc, vc, gq2, gk2, gout_c, *, n_batch, seq, ctx_len, with_ctx):
    t_rows = qc.shape[0]
    n_q_blocks = seq // TQ
    ctx_block0 = n_batch * seq // ctx_len
    q_ctx_block0 = n_batch * seq // TQ
    m_rows = GQA_Q_HEADS * TQ

    def q_map(b, t):
        return (jnp.where(t < n_q_blocks, b * n_q_blocks + t, q_ctx_block0 + b), 0)

    return pl.pallas_call(
        functools.partial(_gqa_kernel, n_q_blocks=n_q_blocks, n_k_chunks=seq // TK),
        grid=(n_batch, n_q_blocks + (1 if with_ctx else 0)),
        in_specs=[
            pl.BlockSpec((TQ, D_C), q_map),
            pl.BlockSpec((seq, D_KV_C), lambda b, t: (b, 0)),
            pl.BlockSpec((seq, D_KV_C), lambda b, t: (b, 0)),
            pl.BlockSpec((ctx_len, D_KV_C), lambda b, t: (ctx_block0 + b, 0)),
            pl.BlockSpec((ctx_len, D_KV_C), lambda b, t: (ctx_block0 + b, 0)),
            pl.BlockSpec((1, LANES), lambda b, t: (0, 0)),
            pl.BlockSpec((1, LANES), lambda b, t: (0, 0)),
            pl.BlockSpec((1, D_C), lambda b, t: (0, 0)),
        ],
        out_specs=pl.BlockSpec((TQ, D_C), q_map),
        out_shape=jax.ShapeDtypeStruct((t_rows, D_C), BF16),
        scratch_shapes=[
            pltpu.VMEM((m_rows, LANES), BF16),
            pltpu.VMEM((m_rows, LANES), F32),
            pltpu.VMEM((m_rows, LANES), F32),
            pltpu.VMEM((m_rows, LANES), F32),
        ],
        compiler_params=pltpu.CompilerParams(
            dimension_semantics=("arbitrary", "arbitrary"), vmem_limit_bytes=VMEM_LIMIT),
        name="gqa",
    )(qc, kc, vc, kc, vc, gq2, gk2, gout_c)


def _post_kernel(oa_ref, ob_ref, oc_ref, x_ref, mod_ref, wo_ref, wa_ref, wb_ref, wout_ref,
                 ln1g_ref, ln1b_ref, ln2g_ref, ln2b_ref, o_ref, h_ref, acc_ref):
    def mod(i):
        return mod_ref[0, :, i * D_MODEL:(i + 1) * D_MODEL]

    y = (_dot(oa_ref[...], wo_ref[0:D_A, :]) + _dot(ob_ref[...], wo_ref[D_A:D_A + D_B, :])
         + _dot(oc_ref[...], wo_ref[D_A + D_B:, :]))
    x1 = _layer_norm(ALPHA * x_ref[...] + mod(2) * y) * ln1g_ref[...] + ln1b_ref[...]
    h_ref[...] = (_layer_norm(x1) * (1.0 + mod(4)) + mod(3)).astype(BF16)
    acc_ref[...] = jnp.zeros_like(acc_ref)

    def body(c, carry):
        h = h_ref[...]
        g = _silu(_dot(h, wa_ref[c])) * _dot(h, wb_ref[c])
        acc_ref[...] += _dot(g.astype(BF16), wout_ref[c])
        return carry

    lax.fori_loop(0, D_FF // FF_CHUNK, body, 0)
    o_ref[...] = _layer_norm(ALPHA * x1 + mod(5) * acc_ref[...]) * ln2g_ref[...] + ln2b_ref[...]


def _post(oa, ob, oc, x_all, mod_l, wo, wa, wb, wout, ln1g, ln1b, ln2g, ln2b, *, n_lat, lat_per_batch,
          n_batch, n_tiles):
    t_rows = x_all.shape[0]
    n_ff = D_FF // FF_CHUNK

    def row_map(t):
        return (t, 0)

    def const2(t):
        return (0, 0)

    def const3(t):
        return (0, 0, 0)

    def mod_map(t):
        return (jnp.where(t < n_lat, t // lat_per_batch, n_batch), 0, 0)

    return pl.pallas_call(
        _post_kernel,
        grid=(n_tiles,),
        in_specs=[
            pl.BlockSpec((TM, D_A), row_map),
            pl.BlockSpec((TM, D_B), row_map),
            pl.BlockSpec((TM, D_C), row_map),
            pl.BlockSpec((TM, D_MODEL), row_map),
            pl.BlockSpec((1, 1, N_MOD * D_MODEL), mod_map),
            pl.BlockSpec((D_MODEL, D_MODEL), const2),
            pl.BlockSpec((n_ff, D_MODEL, FF_CHUNK), const3),
            pl.BlockSpec((n_ff, D_MODEL, FF_CHUNK), const3),
            pl.BlockSpec((n_ff, FF_CHUNK, D_MODEL), const3),
            pl.BlockSpec((1, D_MODEL), const2),
            pl.BlockSpec((1, D_MODEL), const2),
            pl.BlockSpec((1, D_MODEL), const2),
            pl.BlockSpec((1, D_MODEL), const2),
        ],
        out_specs=pl.BlockSpec((TM, D_MODEL), row_map),
        out_shape=jax.ShapeDtypeStruct((t_rows, D_MODEL), F32),
        scratch_shapes=[pltpu.VMEM((TM, D_MODEL), BF16), pltpu.VMEM((TM, D_MODEL), F32)],
        compiler_params=pltpu.CompilerParams(
            dimension_semantics=("arbitrary",), vmem_limit_bytes=VMEM_LIMIT),
        name="post",
    )(oa, ob, oc, x_all, mod_l, wo, wa, wb, wout, ln1g, ln1b, ln2g, ln2b)


def _rope_tables(seq):
    t = jnp.arange(seq, dtype=jnp.int32)
    row = (t // GRID_W).astype(F32)
    col = (t % GRID_W).astype(F32)
    n_freq = HEAD_DIM // 4
    inv = 1.0 / (ROPE_THETA ** (jnp.arange(n_freq, dtype=F32) / n_freq))
    ang = jnp.stack([row[:, None] * inv, col[:, None] * inv], axis=1)
    cos = jnp.cos(ang)[:, :, None, :]
    sin = jnp.sin(ang)[:, :, None, :]
    cos_h = jnp.broadcast_to(cos, (seq, 2, 2, n_freq)).reshape(seq, HEAD_DIM)
    sign = jnp.array([-1.0, 1.0], F32)[None, None, :, None]
    sin_h = jnp.broadcast_to(sin * sign, (seq, 2, 2, n_freq)).reshape(seq, HEAD_DIM)
    cos_t = jnp.concatenate([jnp.tile(cos_h, (1, 2)), jnp.ones((TM, LANES), F32)], 0)
    sin_t = jnp.concatenate([jnp.tile(sin_h, (1, 2)), jnp.zeros((TM, LANES), F32)], 0)
    return cos_t, sin_t


def _na_bias_tables(rpb):
    depth = rpb.shape[0]
    c_idx = np.arange(GRID_W)
    col_start = np.clip(c_idx - NA_WIN_C // 2, 0, GRID_W - NA_WIN_C)
    kc = np.arange(GRID_W)
    valid = (kc[None, :] >= col_start[:, None]) & (kc[None, :] < col_start[:, None] + NA_WIN_C)
    dc = np.clip(kc[None, :] - c_idx[:, None] + (NA_WIN_C - 1), 0, 2 * NA_WIN_C - 2)
    dr = np.arange(NA_WIN_R)[:, None] + np.arange(NA_WIN_R)[None, :]
    tab = rpb[:, :, dr][:, :, :, :, dc]
    tab = jnp.where(jnp.asarray(valid)[None, None, None, None], tab, MASK_VALUE)
    tab = jnp.transpose(tab, (0, 1, 2, 4, 3, 5))
    tab = tab.reshape(depth, NA_HEADS // 2, 2, NA_WIN_R, GRID_W, NA_WIN_R * GRID_W)
    tab = jnp.transpose(tab, (0, 1, 3, 2, 4, 5))
    return tab.reshape(depth, NA_HEADS // 2, NA_WIN_R, 2 * GRID_W, NA_WIN_R * GRID_W)


def kernel(x, c, ctx, c_ctx, w_mod, b_mod, w_in, rpb, w_s, b_s, g_sgu, g_q, g_k, g_out, w_o,
           ln1_g, ln1_b, w_ffn_in, w_ffn_out, ln2_g, ln2_b):
    n_batch, seq, _ = x.shape
    ctx_len = ctx.shape[1]
    depth = w_in.shape[0]
    assert seq % TM == 0 and (n_batch * ctx_len) % TM == 0 and ctx_len == TQ and seq % TK == 0
    assert n_batch + 1 <= 8
    n_lat = n_batch * seq // TM
    n_tiles = n_lat + n_batch * ctx_len // TM
    lat_per_batch = seq // TM

    x_all = jnp.concatenate([x.reshape(n_batch * seq, D_MODEL), ctx.reshape(n_batch * ctx_len, D_MODEL)], 0)

    c_all = jnp.zeros((8, D_MODEL), F32).at[:n_batch].set(c).at[n_batch].set(c_ctx)
    mod = _modulation(c_all, w_mod, b_mod)[:, :n_batch + 1].reshape(depth, n_batch + 1, 1, N_MOD * D_MODEL)

    perm_c = np.concatenate([np.arange(h * HEAD_DIM, (h + 1) * HEAD_DIM) for h in GQA_HEAD_ORDER])
    in_cols = np.concatenate([np.arange(OFF_QC), OFF_QC + perm_c, np.arange(OFF_KC, D_IN)])
    w_in_b = w_in[:, :, in_cols].astype(BF16)
    mix_rows = np.concatenate([np.arange(D_A + D_B), D_A + D_B + perm_c])
    w_o_b = w_o[:, mix_rows, :].astype(BF16)
    g_out_p = g_out[:, mix_rows]
    n_ff = D_FF // FF_CHUNK
    w_a = jnp.transpose(w_ffn_in[:, :, :D_FF].reshape(depth, D_MODEL, n_ff, FF_CHUNK), (0, 2, 1, 3)).astype(BF16)
    w_b = jnp.transpose(w_ffn_in[:, :, D_FF:].reshape(depth, D_MODEL, n_ff, FF_CHUNK), (0, 2, 1, 3)).astype(BF16)
    w_out = w_ffn_out.reshape(depth, n_ff, FF_CHUNK, D_MODEL).astype(BF16)
    w_s_cat = jnp.transpose(w_s, (0, 2, 1, 3)).reshape(depth, SG_CHUNK, SG_GROUPS * SG_CHUNK).astype(BF16)
    b_s_full = jnp.repeat(jnp.transpose(b_s, (0, 2, 1)), HEAD_DIM, axis=2)
    g_q2 = jnp.tile(g_q, (1, 2)).reshape(depth, 1, LANES)
    g_k2 = jnp.tile(g_k, (1, 2)).reshape(depth, 1, LANES)
    cos_t, sin_t = _rope_tables(seq)
    bias = _na_bias_tables(rpb)

    def row(v):
        return v.reshape(1, -1)

    for l in range(depth):
        with_ctx = l < depth - 1
        qa, ka, va, ob, qc, kc, vc = _inproj(
            x_all, mod[l], w_in_b[l], cos_t, sin_t, w_s_cat[l], b_s_full[l], row(g_sgu[l]), g_q2[l], g_k2[l],
            row(g_out_p[l, D_A:D_A + D_B]),
            n_lat=n_lat, lat_per_batch=lat_per_batch, n_batch=n_batch, rope_lat_tiles=seq // TM)
        oa = _na(qa, ka, va, bias[l], row(g_out_p[l, :D_A]),
                 n_batch=n_batch, seq=seq, ctx_len=ctx_len, with_ctx=with_ctx)
        oc = _gqa(qc, kc, vc, g_q2[l], g_k2[l], row(g_out_p[l, D_A + D_B:]),
                  n_batch=n_batch, seq=seq, ctx_len=ctx_len, with_ctx=with_ctx)
        x_all = _post(oa, ob, oc, x_all, mod[l], w_o_b[l], w_a[l], w_b[l], w_out[l],
                      row(ln1_g[l]), row(ln1_b[l]), row(ln2_g[l]), row(ln2_b[l]),
                      n_lat=n_lat, lat_per_batch=lat_per_batch, n_batch=n_batch,
                      n_tiles=n_tiles if with_ctx else n_lat)
    return x_all[:n_batch * seq].reshape(n_batch, seq, D_MODEL)
```

```python
import functools
import math

import jax
import jax.numpy as jnp
import numpy as np
from jax import lax
from jax.experimental import pallas as pl
from jax.experimental.pallas import tpu as pltpu

F32 = jnp.float32
BF16 = jnp.bfloat16

D_MODEL = 1024
HEAD_DIM = 64
GRID_W = 64
NA_HEADS = 6
NA_WIN_R = 8
NA_WIN_C = 16
SG_GROUPS = 4
SG_CHUNK = 128
GQA_Q_HEADS = 6
GQA_KV_HEADS = 2
ROPE_THETA = 10000.0
MODEL_DEPTH = 4

D_A = NA_HEADS * HEAD_DIM
D_B = SG_GROUPS * HEAD_DIM
D_C = GQA_Q_HEADS * HEAD_DIM
D_KV_C = GQA_KV_HEADS * HEAD_DIM
D_IN = 3 * D_A + 2 * D_B + D_C + 2 * D_KV_C
D_FF = int(math.ceil(8 * D_MODEL / 3 / 256)) * 256
N_MOD = 6
ALPHA = (2 * MODEL_DEPTH) ** 0.25
LN_EPS = 1e-6
ATTN_SCALE = HEAD_DIM ** -0.5
MASK_VALUE = -1e30
LOG2_E = 1.4426950408889634
SAFE_SCORE_BOUND = 57.0

OFF_QA, OFF_KA, OFF_VA = 0, D_A, 2 * D_A
OFF_ZB = 3 * D_A
OFF_QC = OFF_ZB + 2 * D_B
OFF_KC = OFF_QC + D_C
OFF_VC = OFF_KC + D_KV_C

LANES = 128
TM = 512
TQ = 256
TK = 512
NA_ROWS_PER_STEP = TQ // GRID_W
NA_UNION_ROWS = 12
FF_CHUNK = 256
MOD_TN = 1536
VMEM_LIMIT = 56 * 1024 * 1024

GQA_HEAD_ORDER = (0, 3, 1, 4, 2, 5)


def _dot(a, b):
    return jnp.dot(a, b, preferred_element_type=F32)


def _dot_nt(a, b):
    return lax.dot_general(a, b, (((1,), (1,)), ((), ())), preferred_element_type=F32)


def _layer_norm(x):
    mu = jnp.mean(x, -1, keepdims=True)
    xc = x - mu
    var = jnp.mean(xc * xc, -1, keepdims=True)
    return xc * lax.rsqrt(var + LN_EPS)


def _rms(x):
    return x * lax.rsqrt(jnp.mean(x * x, -1, keepdims=True) + LN_EPS)


def _gelu_tanh(x):
    return x * (0.5 * (1.0 + jnp.tanh(0.7978845608028654 * (x + 0.044715 * (x * x * x)))))


def _silu(x):
    return x * (1.0 / (1.0 + jnp.exp(-x)))


def _low_half(shape):
    return lax.broadcasted_iota(jnp.int32, shape, 1) < HEAD_DIM


def _pair_rms(xg, gain):
    lo = _low_half(xg.shape)
    sq = xg * xg
    s_lo = jnp.sum(jnp.where(lo, sq, 0.0), -1, keepdims=True)
    s_hi = jnp.sum(jnp.where(lo, 0.0, sq), -1, keepdims=True)
    r = jnp.where(lo, lax.rsqrt(s_lo * (1.0 / HEAD_DIM) + LN_EPS),
                  lax.rsqrt(s_hi * (1.0 / HEAD_DIM) + LN_EPS))
    return xg * r * gain


def _rope(xn, cos_t, sin_t):
    lane = lax.broadcasted_iota(jnp.int32, xn.shape, 1)
    first = (lane % 32) < 16
    partner = jnp.where(first, pltpu.roll(xn, LANES - 16, 1), pltpu.roll(xn, 16, 1))
    return xn * cos_t + partner * sin_t


def _mod_kernel(c_ref, w_ref, b_ref, o_ref):
    sc = _silu(c_ref[...]).astype(BF16)
    o_ref[...] = _dot(sc, w_ref[...].astype(BF16)) + b_ref[...]


def _modulation(c_all, w_mod, b_mod):
    depth = w_mod.shape[0]
    n_out = w_mod.shape[2]
    return pl.pallas_call(
        _mod_kernel,
        grid=(depth, n_out // MOD_TN),
        in_specs=[
            pl.BlockSpec((8, D_MODEL), lambda l, j: (0, 0)),
            pl.BlockSpec((None, D_MODEL, MOD_TN), lambda l, j: (l, 0, j)),
            pl.BlockSpec((None, 1, MOD_TN), lambda l, j: (l, 0, j)),
        ],
        out_specs=pl.BlockSpec((None, 8, MOD_TN), lambda l, j: (l, 0, j)),
        out_shape=jax.ShapeDtypeStruct((depth, 8, n_out), F32),
        compiler_params=pltpu.CompilerParams(
            dimension_semantics=("arbitrary", "arbitrary"), vmem_limit_bytes=VMEM_LIMIT),
        name="modulation",
    )(c_all, w_mod, b_mod.reshape(depth, 1, n_out))


def _inproj_kernel(x_ref, mod_ref, w_ref, cos_ref, sin_ref, wsc_ref, bsg_ref, gsgu_ref,
                   gq_ref, gk_ref, gob_ref,
                   qa_ref, ka_ref, va_ref, ob_ref, qc_ref, kc_ref, vc_ref, y_ref):
    shift = mod_ref[0, :, 0:D_MODEL]
    scale = mod_ref[0, :, D_MODEL:2 * D_MODEL]
    h = (_layer_norm(x_ref[...]) * (1.0 + scale) + shift).astype(BF16)
    y_ref[...] = _dot(h, w_ref[...])

    def proj(off, width):
        return y_ref[:, off:off + width]

    qa_ref[...] = (proj(OFF_QA, D_A) * ATTN_SCALE).astype(BF16)
    ka_ref[...] = proj(OFF_KA, D_A).astype(BF16)
    va_ref[...] = proj(OFF_VA, D_A).astype(BF16)

    u = _gelu_tanh(proj(OFF_ZB, D_B))
    v = (_layer_norm(_gelu_tanh(proj(OFF_ZB + D_B, D_B))) * gsgu_ref[...]).astype(BF16)
    lane_group = lax.broadcasted_iota(jnp.int32, (SG_CHUNK, D_B), 1) // HEAD_DIM
    for c in range(TM // SG_CHUNK):
        rows = slice(c * SG_CHUNK, (c + 1) * SG_CHUNK)
        vch = v[rows]
        rhs = jnp.concatenate(
            [jnp.where(lane_group == g, vch, jnp.zeros_like(vch)) for g in range(SG_GROUPS)], 0)
        mixed = _dot(wsc_ref[...], rhs) + bsg_ref[...]
        ob_ref[rows, :] = (_rms(u[rows] * mixed) * gob_ref[...]).astype(BF16)

    cos_t = cos_ref[...]
    sin_t = sin_ref[...]
    for p in range(D_C // LANES):
        xq = proj(OFF_QC + p * LANES, LANES)
        xq = _rope(_pair_rms(xq, gq_ref[...]), cos_t, sin_t)
        qc_ref[:, p * LANES:(p + 1) * LANES] = (xq * (ATTN_SCALE * LOG2_E)).astype(BF16)
    xk = _rope(_pair_rms(proj(OFF_KC, D_KV_C), gk_ref[...]), cos_t, sin_t)
    kc_ref[...] = xk.astype(BF16)
    vc_ref[...] = proj(OFF_VC, D_KV_C).astype(BF16)


def _layer_spec(layer, shape):
    zeros = (0,) * len(shape)
    return pl.BlockSpec((None,) + tuple(shape), lambda *_: (layer,) + zeros)


def _inproj(x_all, mod, w_in_b, cos_t, sin_t, wsc, bsg, gsgu, gq2, gk2, gob, *, layer, n_lat, lat_per_batch,
            n_batch, rope_lat_tiles):
    t_rows = x_all.shape[0]
    n_tiles = t_rows // TM

    def row_map(t):
        return (t, 0)

    def mod_map(t):
        return (layer, jnp.where(t < n_lat, t // lat_per_batch, n_batch), 0, 0)

    def rope_map(t):
        return (jnp.where(t < n_lat, t % rope_lat_tiles, rope_lat_tiles), 0)

    widths = (D_A, D_A, D_A, D_B, D_C, D_KV_C, D_KV_C)
    return pl.pallas_call(
        _inproj_kernel,
        grid=(n_tiles,),
        in_specs=[
            pl.BlockSpec((TM, D_MODEL), row_map),
            pl.BlockSpec((None, 1, 1, N_MOD * D_MODEL), mod_map),
            _layer_spec(layer, (D_MODEL, D_IN)),
            pl.BlockSpec((TM, LANES), rope_map),
            pl.BlockSpec((TM, LANES), rope_map),
            _layer_spec(layer, (SG_CHUNK, SG_GROUPS * SG_CHUNK)),
            _layer_spec(layer, (SG_CHUNK, D_B)),
            _layer_spec(layer, (1, D_B)),
            _layer_spec(layer, (1, LANES)),
            _layer_spec(layer, (1, LANES)),
            _layer_spec(layer, (1, D_B)),
        ],
        out_specs=[pl.BlockSpec((TM, w), row_map) for w in widths],
        out_shape=[jax.ShapeDtypeStruct((t_rows, w), BF16) for w in widths],
        scratch_shapes=[pltpu.VMEM((TM, D_IN), F32)],
        compiler_params=pltpu.CompilerParams(
            dimension_semantics=("arbitrary",), vmem_limit_bytes=VMEM_LIMIT),
        name="inproj",
    )(x_all, mod, w_in_b, cos_t, sin_t, wsc, bsg, gsgu, gq2, gk2, gob)


def _split_heads(qg):
    lo = _low_half(qg.shape)
    zero = jnp.zeros_like(qg)
    return jnp.concatenate([jnp.where(lo, qg, zero), jnp.where(lo, zero, qg)], 0)


def _join_heads(o, m):
    return jnp.where(_low_half((m, LANES)), o[0:m], o[m:2 * m])


def _write_merged(o_ref, rows, outs, gout_ref):
    width = len(outs) * LANES
    ss = outs[0] * outs[0]
    for o in outs[1:]:
        ss = ss + o * o
    r = lax.rsqrt(jnp.sum(ss, -1, keepdims=True) * (1.0 / width) + LN_EPS)
    for p, o in enumerate(outs):
        cols = slice(p * LANES, (p + 1) * LANES)
        o_ref[rows, cols] = (o * r * gout_ref[:, cols]).astype(BF16)


def _na_kernel(q_ref, k_ref, v_ref, kx_ref, vx_ref, bias_ref, gout_ref, o_ref, *, grid_rows, n_row_blocks):
    t = pl.program_id(1)
    n_pairs = D_A // LANES
    span = NA_UNION_ROWS * GRID_W

    @pl.when(t < n_row_blocks)
    def _latent():
        first = jnp.clip(t * NA_ROWS_PER_STEP - NA_WIN_R // 2, 0, grid_rows - NA_UNION_ROWS)
        start = pl.multiple_of(first * GRID_W, GRID_W)
        outs = []
        for p in range(n_pairs):
            cols = slice(p * LANES, (p + 1) * LANES)
            lhs = _split_heads(q_ref[:, cols])
            s_loc = _dot_nt(lhs, k_ref[pl.ds(start, span), cols]) + bias_ref[0, p]
            s_ctx = _dot_nt(lhs, kx_ref[:, cols])
            m = jnp.maximum(jnp.max(s_loc, -1, keepdims=True), jnp.max(s_ctx, -1, keepdims=True))
            e_loc = jnp.exp(s_loc - m)
            e_ctx = jnp.exp(s_ctx - m)
            l = jnp.sum(e_loc, -1, keepdims=True) + jnp.sum(e_ctx, -1, keepdims=True)
            o = _dot(e_loc.astype(BF16), v_ref[pl.ds(start, span), cols]) \
                + _dot(e_ctx.astype(BF16), vx_ref[:, cols])
            outs.append(_join_heads(o * (1.0 / l), TQ))
        _write_merged(o_ref, slice(0, TQ), outs, gout_ref)

    @pl.when(t == n_row_blocks)
    def _context():
        outs = []
        for p in range(n_pairs):
            cols = slice(p * LANES, (p + 1) * LANES)
            lhs = _split_heads(q_ref[:, cols])
            s = _dot_nt(lhs, kx_ref[:, cols])
            e = jnp.exp(s - jnp.max(s, -1, keepdims=True))
            o = _dot(e.astype(BF16), vx_ref[:, cols]) * (1.0 / jnp.sum(e, -1, keepdims=True))
            outs.append(_join_heads(o, TQ))
        _write_merged(o_ref, slice(0, TQ), outs, gout_ref)


def _na(qa, ka, va, bias, gout_a, *, layer, n_batch, seq, ctx_len, with_ctx):
    grid_rows = seq // GRID_W
    n_row_blocks = seq // TQ
    assert grid_rows >= NA_UNION_ROWS + NA_ROWS_PER_STEP
    ctx_block0 = n_batch * seq // ctx_len
    q_ctx_block0 = n_batch * seq // TQ
    out_rows = qa.shape[0] if with_ctx else n_batch * seq

    def q_map(b, t):
        return (jnp.where(t < n_row_blocks, b * n_row_blocks + t, q_ctx_block0 + b), 0)

    def bias_map(b, t):
        return (layer, (t > 0).astype(jnp.int32) + (t >= n_row_blocks - 1).astype(jnp.int32), 0, 0, 0)

    return pl.pallas_call(
        functools.partial(_na_kernel, grid_rows=grid_rows, n_row_blocks=n_row_blocks),
        grid=(n_batch, n_row_blocks + (1 if with_ctx else 0)),
        in_specs=[
            pl.BlockSpec((TQ, D_A), q_map),
            pl.BlockSpec((seq, D_A), lambda b, t: (b, 0)),
            pl.BlockSpec((seq, D_A), lambda b, t: (b, 0)),
            pl.BlockSpec((ctx_len, D_A), lambda b, t: (ctx_block0 + b, 0)),
            pl.BlockSpec((ctx_len, D_A), lambda b, t: (ctx_block0 + b, 0)),
            pl.BlockSpec((None, 1) + bias.shape[2:], bias_map),
            _layer_spec(layer, (1, D_A)),
        ],
        out_specs=pl.BlockSpec((TQ, D_A), q_map),
        out_shape=jax.ShapeDtypeStruct((out_rows, D_A), BF16),
        compiler_params=pltpu.CompilerParams(
            dimension_semantics=("arbitrary", "arbitrary"), vmem_limit_bytes=VMEM_LIMIT),
        name="na",
    )(qa, ka, va, ka, va, bias, gout_a)


def _gqa_kernel(q_ref, k_ref, v_ref, kx_ref, vx_ref, gq_ref, gk_ref, gout_ref, o_ref,
                lhs_ref, m_ref, l_ref, acc_ref, *, n_q_blocks, n_k_chunks):
    t = pl.program_id(1)
    n_groups = D_C // LANES
    m_rows = GQA_Q_HEADS * TQ

    for j in range(GQA_KV_HEADS):
        for g in range(n_groups):
            qg = q_ref[:, g * LANES:(g + 1) * LANES]
            lo = _low_half(qg.shape)
            keep = lo if j == 0 else jnp.logical_not(lo)
            lhs_ref[(j * n_groups + g) * TQ:(j * n_groups + g + 1) * TQ, :] = jnp.where(keep, qg, jnp.zeros_like(qg))

    score_bound = (HEAD_DIM * ATTN_SCALE * LOG2_E) * jnp.max(jnp.abs(gq_ref[...])) * jnp.max(jnp.abs(gk_ref[...]))
    bounded = score_bound <= SAFE_SCORE_BOUND

    @pl.when(bounded)
    def _fixed_shift():
        def chunk(k_c, v_c):
            p = jnp.exp2(_dot_nt(lhs_ref[...], k_c) - score_bound)
            l_part = p[:, 0:LANES]
            for i in range(1, k_c.shape[0] // LANES):
                l_part = l_part + p[:, i * LANES:(i + 1) * LANES]
            return l_part, _dot(p.astype(BF16), v_c)

        l0, a0 = chunk(kx_ref[...], vx_ref[...])
        l_ref[...] = l0
        acc_ref[...] = a0

        @pl.when(t < n_q_blocks)
        def _latent_keys():
            def body(c, carry):
                start = pl.multiple_of(c * TK, TK)
                l_c, a_c = chunk(k_ref[pl.ds(start, TK), :], v_ref[pl.ds(start, TK), :])
                l_ref[...] += l_c
                acc_ref[...] += a_c
                return carry

            lax.fori_loop(0, n_k_chunks, body, 0, unroll=2)

        acc_ref[...] = acc_ref[...] * (1.0 / jnp.sum(l_ref[...], -1, keepdims=True))

    @pl.when(jnp.logical_not(bounded))
    def _running_max():
        s = _dot_nt(lhs_ref[...], kx_ref[...])
        m0 = jnp.max(s, -1, keepdims=True)
        e = jnp.exp2(s - m0)
        m_ref[...] = jnp.broadcast_to(m0, (m_rows, LANES))
        l_ref[...] = jnp.broadcast_to(jnp.sum(e, -1, keepdims=True), (m_rows, LANES))
        acc_ref[...] = _dot(e.astype(BF16), vx_ref[...])

        @pl.when(t < n_q_blocks)
        def _latent_keys():
            def body(c, carry):
                start = pl.multiple_of(c * TK, TK)
                s = _dot_nt(lhs_ref[...], k_ref[pl.ds(start, TK), :])
                m_prev = m_ref[:, 0:1]
                m_new = jnp.maximum(m_prev, jnp.max(s, -1, keepdims=True))
                alpha = jnp.exp2(m_prev - m_new)
                e = jnp.exp2(s - m_new)
                l_new = alpha * l_ref[:, 0:1] + jnp.sum(e, -1, keepdims=True)
                acc_ref[...] = alpha * acc_ref[...] + _dot(e.astype(BF16), v_ref[pl.ds(start, TK), :])
                m_ref[...] = jnp.broadcast_to(m_new, (m_rows, LANES))
                l_ref[...] = jnp.broadcast_to(l_new, (m_rows, LANES))
                return carry

            lax.fori_loop(0, n_k_chunks, body, 0)

        acc_ref[...] = acc_ref[...] * (1.0 / l_ref[:, 0:1])

    lo = _low_half((TQ, LANES))
    outs = [jnp.where(lo, acc_ref[g * TQ:(g + 1) * TQ, :], acc_ref[(n_groups + g) * TQ:(n_groups + g + 1) * TQ, :])
            for g in range(n_groups)]
    _write_merged(o_ref, slice(0, TQ), outs, gout_ref)


def _gqa(qc, kc, vc, gq2, gk2, gout_c, *, layer, n_batch, seq, ctx_len, with_ctx):
    out_rows = qc.shape[0] if with_ctx else n_batch * seq
    n_q_blocks = seq // TQ
    ctx_block0 = n_batch * seq // ctx_len
    q_ctx_block0 = n_batch * seq // TQ
    m_rows = GQA_Q_HEADS * TQ

    def q_map(b, t):
        return (jnp.where(t < n_q_blocks, b * n_q_blocks + t, q_ctx_block0 + b), 0)

    return pl.pallas_call(
        functools.partial(_gqa_kernel, n_q_blocks=n_q_blocks, n_k_chunks=seq // TK),
        grid=(n_batch, n_q_blocks + (1 if with_ctx else 0)),
        in_specs=[
            pl.BlockSpec((TQ, D_C), q_map),
            pl.BlockSpec((seq, D_KV_C), lambda b, t: (b, 0)),
            pl.BlockSpec((seq, D_KV_C), lambda b, t: (b, 0)),
            pl.BlockSpec((ctx_len, D_KV_C), lambda b, t: (ctx_block0 + b, 0)),
            pl.BlockSpec((ctx_len, D_KV_C), lambda b, t: (ctx_block0 + b, 0)),
            _layer_spec(layer, (1, LANES)),
            _layer_spec(layer, (1, LANES)),
            _layer_spec(layer, (1, D_C)),
        ],
        out_specs=pl.BlockSpec((TQ, D_C), q_map),
        out_shape=jax.ShapeDtypeStruct((out_rows, D_C), BF16),
        scratch_shapes=[
            pltpu.VMEM((m_rows, LANES), BF16),
            pltpu.VMEM((m_rows, LANES), F32),
            pltpu.VMEM((m_rows, LANES), F32),
            pltpu.VMEM((m_rows, LANES), F32),
        ],
        compiler_params=pltpu.CompilerParams(
            dimension_semantics=("arbitrary", "arbitrary"), vmem_limit_bytes=VMEM_LIMIT),
        name="gqa",
    )(qc, kc, vc, kc, vc, gq2, gk2, gout_c)


def _post_kernel(oa_ref, ob_ref, oc_ref, x_ref, mod_ref, wo_ref, wffn_ref, wout_ref,
                 ln1g_ref, ln1b_ref, ln2g_ref, ln2b_ref, o_ref, h_ref, g_ref):
    def mod(i):
        return mod_ref[0, :, i * D_MODEL:(i + 1) * D_MODEL]

    y = (_dot(oa_ref[...], wo_ref[0:D_A, :]) + _dot(ob_ref[...], wo_ref[D_A:D_A + D_B, :])
         + _dot(oc_ref[...], wo_ref[D_A + D_B:, :]))
    x1 = _layer_norm(ALPHA * x_ref[...] + mod(2) * y) * ln1g_ref[...] + ln1b_ref[...]
    h_ref[...] = (_layer_norm(x1) * (1.0 + mod(4)) + mod(3)).astype(BF16)
    for c in range(D_FF // FF_CHUNK):
        cols = slice(c * FF_CHUNK, (c + 1) * FF_CHUNK)
        gate = _dot(h_ref[...], wffn_ref[:, cols])
        up = _dot(h_ref[...], wffn_ref[:, D_FF + c * FF_CHUNK:D_FF + (c + 1) * FF_CHUNK])
        g_ref[:, cols] = (_silu(gate) * up).astype(BF16)
    ff = _dot(g_ref[...], wout_ref[...])
    o_ref[...] = _layer_norm(ALPHA * x1 + mod(5) * ff) * ln2g_ref[...] + ln2b_ref[...]


def _post(oa, ob, oc, x_all, mod, wo, wffn, wout, ln1g, ln1b, ln2g, ln2b, *, layer, n_lat, lat_per_batch,
          n_batch, n_tiles):
    def row_map(t):
        return (t, 0)

    def const2(t):
        return (0, 0)

    def mod_map(t):
        return (layer, jnp.where(t < n_lat, t // lat_per_batch, n_batch), 0, 0)

    def resident(shape):
        return pl.BlockSpec((None,) + shape, lambda t: (layer, 0, 0), pipeline_mode=pl.Buffered(1))

    return pl.pallas_call(
        _post_kernel,
        grid=(n_tiles,),
        in_specs=[
            pl.BlockSpec((TM, D_A), row_map),
            pl.BlockSpec((TM, D_B), row_map),
            pl.BlockSpec((TM, D_C), row_map),
            pl.BlockSpec((TM, D_MODEL), row_map),
            pl.BlockSpec((None, 1, 1, N_MOD * D_MODEL), mod_map),
            resident((D_MODEL, D_MODEL)),
            resident((D_MODEL, 2 * D_FF)),
            resident((D_FF, D_MODEL)),
            _layer_spec(layer, (1, D_MODEL)),
            _layer_spec(layer, (1, D_MODEL)),
            _layer_spec(layer, (1, D_MODEL)),
            _layer_spec(layer, (1, D_MODEL)),
        ],
        out_specs=pl.BlockSpec((TM, D_MODEL), row_map),
        out_shape=jax.ShapeDtypeStruct((n_tiles * TM, D_MODEL), F32),
        scratch_shapes=[pltpu.VMEM((TM, D_MODEL), BF16), pltpu.VMEM((TM, D_FF), BF16)],
        compiler_params=pltpu.CompilerParams(
            dimension_semantics=("arbitrary",), vmem_limit_bytes=VMEM_LIMIT),
        name="post",
    )(oa, ob, oc, x_all, mod, wo, wffn, wout, ln1g, ln1b, ln2g, ln2b)


def _rope_tables(seq):
    t = jnp.arange(seq, dtype=jnp.int32)
    row = (t // GRID_W).astype(F32)
    col = (t % GRID_W).astype(F32)
    n_freq = HEAD_DIM // 4
    inv = 1.0 / (ROPE_THETA ** (jnp.arange(n_freq, dtype=F32) / n_freq))
    ang = jnp.stack([row[:, None] * inv, col[:, None] * inv], axis=1)
    cos = jnp.cos(ang)[:, :, None, :]
    sin = jnp.sin(ang)[:, :, None, :]
    cos_h = jnp.broadcast_to(cos, (seq, 2, 2, n_freq)).reshape(seq, HEAD_DIM)
    sign = jnp.array([-1.0, 1.0], F32)[None, None, :, None]
    sin_h = jnp.broadcast_to(sin * sign, (seq, 2, 2, n_freq)).reshape(seq, HEAD_DIM)
    cos_t = jnp.concatenate([jnp.tile(cos_h, (1, 2)), jnp.ones((TM, LANES), F32)], 0)
    sin_t = jnp.concatenate([jnp.tile(sin_h, (1, 2)), jnp.zeros((TM, LANES), F32)], 0)
    return cos_t, sin_t


def _na_bias_tables(rpb):
    depth = rpb.shape[0]
    c_idx = np.arange(GRID_W)
    col_start = np.clip(c_idx - NA_WIN_C // 2, 0, GRID_W - NA_WIN_C)
    kc = np.arange(GRID_W)
    col_ok = (kc[None, :] >= col_start[:, None]) & (kc[None, :] < col_start[:, None] + NA_WIN_C)
    pad = GRID_W - NA_WIN_C
    padded = jnp.pad(rpb, ((0, 0), (0, 0), (0, 0), (pad, pad)))
    toep = jnp.stack([padded[..., GRID_W - 1 - c:2 * GRID_W - 1 - c] for c in range(GRID_W)], axis=-2)
    toep = jnp.where(jnp.asarray(col_ok), toep, MASK_VALUE)
    lead = np.array([0, NA_WIN_R // 2, NA_UNION_ROWS - NA_ROWS_PER_STEP])[:, None, None]
    i = np.arange(NA_ROWS_PER_STEP)[None, :, None]
    j = np.arange(NA_UNION_ROWS)[None, None, :]
    win_first = np.stack([np.zeros((NA_ROWS_PER_STEP, 1), np.int64), np.arange(NA_ROWS_PER_STEP)[:, None],
                          np.full((NA_ROWS_PER_STEP, 1), NA_UNION_ROWS - NA_WIN_R)])
    row_ok = (j >= win_first) & (j < win_first + NA_WIN_R)
    dr = np.clip(j - lead - i + (NA_WIN_R - 1), 0, 2 * NA_WIN_R - 2)
    tab = jnp.take(toep, jnp.asarray(dr.reshape(-1)), axis=2)
    tab = tab.reshape(depth, NA_HEADS // 2, 2, 3, NA_ROWS_PER_STEP, NA_UNION_ROWS, GRID_W, GRID_W)
    tab = jnp.where(jnp.asarray(row_ok)[:, :, :, None, None], tab, MASK_VALUE)
    tab = jnp.transpose(tab, (0, 3, 1, 2, 4, 6, 5, 7))
    return tab.reshape(depth, 3, NA_HEADS // 2, 2 * TQ, NA_UNION_ROWS * GRID_W)


def kernel(x, c, ctx, c_ctx, w_mod, b_mod, w_in, rpb, w_s, b_s, g_sgu, g_q, g_k, g_out, w_o,
           ln1_g, ln1_b, w_ffn_in, w_ffn_out, ln2_g, ln2_b):
    n_batch, seq, _ = x.shape
    ctx_len = ctx.shape[1]
    depth = w_in.shape[0]
    assert seq % TM == 0 and (n_batch * ctx_len) % TM == 0 and ctx_len == TQ and seq % TK == 0
    assert n_batch + 1 <= 8
    n_lat = n_batch * seq // TM
    n_tiles = n_lat + n_batch * ctx_len // TM
    lat_per_batch = seq // TM

    x_all = jnp.concatenate([x.reshape(n_batch * seq, D_MODEL), ctx.reshape(n_batch * ctx_len, D_MODEL)], 0)

    c_all = jnp.zeros((8, D_MODEL), F32).at[:n_batch].set(c).at[n_batch].set(c_ctx)
    mod = _modulation(c_all, w_mod, b_mod)[:, :n_batch + 1].reshape(depth, n_batch + 1, 1, N_MOD * D_MODEL)

    def reorder_gqa_heads(a, axis, off):
        parts = [lax.slice_in_dim(a, 0, off, axis=axis)]
        parts += [lax.slice_in_dim(a, off + h * HEAD_DIM, off + (h + 1) * HEAD_DIM, axis=axis)
                  for h in GQA_HEAD_ORDER]
        parts.append(lax.slice_in_dim(a, off + D_C, a.shape[axis], axis=axis))
        return jnp.concatenate(parts, axis)

    def rows(v):
        return v.reshape(depth, 1, -1)

    w_in_b = reorder_gqa_heads(w_in, 2, OFF_QC).astype(BF16)
    w_o_b = reorder_gqa_heads(w_o, 1, D_A + D_B).astype(BF16)
    g_out_p = reorder_gqa_heads(g_out, 1, D_A + D_B)
    g_out_a, g_out_b, g_out_c = (rows(g_out_p[:, :D_A]), rows(g_out_p[:, D_A:D_A + D_B]),
                                 rows(g_out_p[:, D_A + D_B:]))
    w_ffn_b = w_ffn_in.astype(BF16)
    w_out_b = w_ffn_out.astype(BF16)
    w_s_cat = jnp.transpose(w_s, (0, 2, 1, 3)).reshape(depth, SG_CHUNK, SG_GROUPS * SG_CHUNK).astype(BF16)
    b_s_full = jnp.repeat(jnp.transpose(b_s, (0, 2, 1)), HEAD_DIM, axis=2)
    g_sgu3 = rows(g_sgu)
    g_q2 = rows(jnp.tile(g_q, (1, 2)))
    g_k2 = rows(jnp.tile(g_k, (1, 2)))
    ln1g, ln1b, ln2g, ln2b = rows(ln1_g), rows(ln1_b), rows(ln2_g), rows(ln2_b)
    cos_t, sin_t = _rope_tables(seq)
    bias = _na_bias_tables(rpb)

    for l in range(depth):
        with_ctx = l < depth - 1
        qa, ka, va, ob, qc, kc, vc = _inproj(
            x_all, mod, w_in_b, cos_t, sin_t, w_s_cat, b_s_full, g_sgu3, g_q2, g_k2, g_out_b,
            layer=l, n_lat=n_lat, lat_per_batch=lat_per_batch, n_batch=n_batch, rope_lat_tiles=seq // TM)
        oa = _na(qa, ka, va, bias, g_out_a, layer=l, n_batch=n_batch, seq=seq, ctx_len=ctx_len, with_ctx=with_ctx)
        oc = _gqa(qc, kc, vc, g_q2, g_k2, g_out_c,
                  layer=l, n_batch=n_batch, seq=seq, ctx_len=ctx_len, with_ctx=with_ctx)
        x_all = _post(oa, ob, oc, x_all, mod, w_o_b, w_ffn_b, w_out_b, ln1g, ln1b, ln2g, ln2b,
                      layer=l, n_lat=n_lat, lat_per_batch=lat_per_batch, n_batch=n_batch,
                      n_tiles=n_tiles if with_ctx else n_lat)
    return x_all.reshape(n_batch, seq, D_MODEL)
```

```python
import functools
import math

import jax
import jax.numpy as jnp
import numpy as np
from jax import lax
from jax.experimental import pallas as pl
from jax.experimental.pallas import tpu as pltpu

F32 = jnp.float32
BF16 = jnp.bfloat16

D_MODEL = 1024
HEAD_DIM = 64
GRID_W = 64
NA_HEADS = 6
NA_WIN_R = 8
NA_WIN_C = 16
SG_GROUPS = 4
SG_CHUNK = 128
GQA_Q_HEADS = 6
GQA_KV_HEADS = 2
ROPE_THETA = 10000.0
MODEL_DEPTH = 4

D_A = NA_HEADS * HEAD_DIM
D_B = SG_GROUPS * HEAD_DIM
D_C = GQA_Q_HEADS * HEAD_DIM
D_KV_C = GQA_KV_HEADS * HEAD_DIM
D_IN = 3 * D_A + 2 * D_B + D_C + 2 * D_KV_C
D_FF = int(math.ceil(8 * D_MODEL / 3 / 256)) * 256
N_MOD = 6
ALPHA = (2 * MODEL_DEPTH) ** 0.25
LN_EPS = 1e-6
ATTN_SCALE = HEAD_DIM ** -0.5
MASK_VALUE = -1e30
LOG2_E = 1.4426950408889634
SAFE_SCORE_BOUND = 57.0

OFF_QA, OFF_KA, OFF_VA = 0, D_A, 2 * D_A
OFF_ZB = 3 * D_A
OFF_QC = OFF_ZB + 2 * D_B
OFF_KC = OFF_QC + D_C
OFF_VC = OFF_KC + D_KV_C

LANES = 128
TM = 512
TQ = 256
TK = 512
GQA_UNROLL = 8
NA_ROWS_PER_STEP = TQ // GRID_W
NA_UNION_ROWS = 12
FF_CHUNK = 256
MOD_TN = 1536
VMEM_LIMIT = 56 * 1024 * 1024

GQA_HEAD_ORDER = (0, 3, 1, 4, 2, 5)


def _dot(a, b):
    return jnp.dot(a, b, preferred_element_type=F32)


def _dot_nt(a, b):
    return lax.dot_general(a, b, (((1,), (1,)), ((), ())), preferred_element_type=F32)


def _layer_norm(x):
    mu = jnp.mean(x, -1, keepdims=True)
    xc = x - mu
    var = jnp.mean(xc * xc, -1, keepdims=True)
    return xc * lax.rsqrt(var + LN_EPS)


def _rms(x):
    return x * lax.rsqrt(jnp.mean(x * x, -1, keepdims=True) + LN_EPS)


def _gelu_tanh(x):
    return x * (0.5 * (1.0 + jnp.tanh(0.7978845608028654 * (x + 0.044715 * (x * x * x)))))


def _silu(x):
    return x * (1.0 / (1.0 + jnp.exp(-x)))


def _low_half(shape):
    return lax.broadcasted_iota(jnp.int32, shape, 1) < HEAD_DIM


def _pair_rms(xg, gain):
    lo = _low_half(xg.shape)
    sq = xg * xg
    s_lo = jnp.sum(jnp.where(lo, sq, 0.0), -1, keepdims=True)
    s_hi = jnp.sum(jnp.where(lo, 0.0, sq), -1, keepdims=True)
    r = jnp.where(lo, lax.rsqrt(s_lo * (1.0 / HEAD_DIM) + LN_EPS),
                  lax.rsqrt(s_hi * (1.0 / HEAD_DIM) + LN_EPS))
    return xg * r * gain


def _rope(xn, cos_t, sin_t):
    lane = lax.broadcasted_iota(jnp.int32, xn.shape, 1)
    first = (lane % 32) < 16
    partner = jnp.where(first, pltpu.roll(xn, LANES - 16, 1), pltpu.roll(xn, 16, 1))
    return xn * cos_t + partner * sin_t


def _mod_kernel(c_ref, w_ref, b_ref, o_ref):
    sc = _silu(c_ref[...]).astype(BF16)
    o_ref[...] = _dot(sc, w_ref[...].astype(BF16)) + b_ref[...]


def _modulation(c_all, w_mod, b_mod):
    depth = w_mod.shape[0]
    n_out = w_mod.shape[2]
    return pl.pallas_call(
        _mod_kernel,
        grid=(depth, n_out // MOD_TN),
        in_specs=[
            pl.BlockSpec((8, D_MODEL), lambda l, j: (0, 0)),
            pl.BlockSpec((None, D_MODEL, MOD_TN), lambda l, j: (l, 0, j)),
            pl.BlockSpec((None, 1, MOD_TN), lambda l, j: (l, 0, j)),
        ],
        out_specs=pl.BlockSpec((None, 8, MOD_TN), lambda l, j: (l, 0, j)),
        out_shape=jax.ShapeDtypeStruct((depth, 8, n_out), F32),
        compiler_params=pltpu.CompilerParams(
            dimension_semantics=("arbitrary", "arbitrary"), vmem_limit_bytes=VMEM_LIMIT),
        name="modulation",
    )(c_all, w_mod, b_mod.reshape(depth, 1, n_out))


def _inproj_kernel(x_ref, mod_ref, w_ref, cos_ref, sin_ref, wsc_ref, bsg_ref, gsgu_ref,
                   gq_ref, gk_ref, gob_ref,
                   qa_ref, ka_ref, va_ref, ob_ref, qc_ref, kc_ref, vc_ref, y_even_ref, y_odd_ref):
    t = pl.program_id(0)

    @pl.when(t == 0)
    def _no_previous_tile():
        y_odd_ref[...] = jnp.zeros_like(y_odd_ref)

    @pl.when(t % 2 == 0)
    def _even():
        _inproj_step(x_ref, mod_ref, w_ref, cos_ref, sin_ref, wsc_ref, bsg_ref, gsgu_ref, gq_ref, gk_ref, gob_ref,
                     qa_ref, ka_ref, va_ref, ob_ref, qc_ref, kc_ref, vc_ref, y_even_ref, y_odd_ref)

    @pl.when(t % 2 == 1)
    def _odd():
        _inproj_step(x_ref, mod_ref, w_ref, cos_ref, sin_ref, wsc_ref, bsg_ref, gsgu_ref, gq_ref, gk_ref, gob_ref,
                     qa_ref, ka_ref, va_ref, ob_ref, qc_ref, kc_ref, vc_ref, y_odd_ref, y_even_ref)


def _inproj_step(x_ref, mod_ref, w_ref, cos_ref, sin_ref, wsc_ref, bsg_ref, gsgu_ref, gq_ref, gk_ref, gob_ref,
                 qa_ref, ka_ref, va_ref, ob_ref, qc_ref, kc_ref, vc_ref, y_new_ref, y_ref):
    shift = mod_ref[0, :, 0:D_MODEL]
    scale = mod_ref[0, :, D_MODEL:2 * D_MODEL]
    h = (_layer_norm(x_ref[...]) * (1.0 + scale) + shift).astype(BF16)
    y_new_ref[...] = _dot(h, w_ref[...])

    def proj(off, width):
        return y_ref[:, off:off + width]

    qa_ref[...] = (proj(OFF_QA, D_A) * ATTN_SCALE).astype(BF16)
    ka_ref[...] = proj(OFF_KA, D_A).astype(BF16)
    va_ref[...] = proj(OFF_VA, D_A).astype(BF16)

    u = _gelu_tanh(proj(OFF_ZB, D_B))
    v = (_layer_norm(_gelu_tanh(proj(OFF_ZB + D_B, D_B))) * gsgu_ref[...]).astype(BF16)
    lane_group = lax.broadcasted_iota(jnp.int32, (SG_CHUNK, D_B), 1) // HEAD_DIM
    for c in range(TM // SG_CHUNK):
        rows = slice(c * SG_CHUNK, (c + 1) * SG_CHUNK)
        vch = v[rows]
        rhs = jnp.concatenate(
            [jnp.where(lane_group == g, vch, jnp.zeros_like(vch)) for g in range(SG_GROUPS)], 0)
        mixed = _dot(wsc_ref[...], rhs) + bsg_ref[...]
        ob_ref[rows, :] = (_rms(u[rows] * mixed) * gob_ref[...]).astype(BF16)

    cos_t = cos_ref[...]
    sin_t = sin_ref[...]
    for p in range(D_C // LANES):
        xq = proj(OFF_QC + p * LANES, LANES)
        xq = _rope(_pair_rms(xq, gq_ref[...]), cos_t, sin_t)
        qc_ref[:, p * LANES:(p + 1) * LANES] = (xq * (ATTN_SCALE * LOG2_E)).astype(BF16)
    xk = _rope(_pair_rms(proj(OFF_KC, D_KV_C), gk_ref[...]), cos_t, sin_t)
    kc_ref[...] = xk.astype(BF16)
    vc_ref[...] = proj(OFF_VC, D_KV_C).astype(BF16)


def _layer_spec(layer, shape):
    zeros = (0,) * len(shape)
    return pl.BlockSpec((None,) + tuple(shape), lambda *_: (layer,) + zeros)


def _inproj(x_all, mod, w_in_b, cos_t, sin_t, wsc, bsg, gsgu, gq2, gk2, gob, *, layer, n_lat, lat_per_batch,
            n_batch, rope_lat_tiles):
    t_rows = x_all.shape[0]
    n_tiles = t_rows // TM

    def in_map(t):
        return (jnp.minimum(t, n_tiles - 1), 0)

    def row_map(t):
        return (jnp.maximum(t - 1, 0), 0)

    def mod_map(t):
        tile = jnp.minimum(t, n_tiles - 1)
        return (layer, jnp.where(tile < n_lat, tile // lat_per_batch, n_batch), 0, 0)

    def rope_map(t):
        tile = jnp.maximum(t - 1, 0)
        return (jnp.where(tile < n_lat, tile % rope_lat_tiles, rope_lat_tiles), 0)

    widths = (D_A, D_A, D_A, D_B, D_C, D_KV_C, D_KV_C)
    return pl.pallas_call(
        _inproj_kernel,
        grid=(n_tiles + 1,),
        in_specs=[
            pl.BlockSpec((TM, D_MODEL), in_map),
            pl.BlockSpec((None, 1, 1, N_MOD * D_MODEL), mod_map),
            _layer_spec(layer, (D_MODEL, D_IN)),
            pl.BlockSpec((TM, LANES), rope_map),
            pl.BlockSpec((TM, LANES), rope_map),
            _layer_spec(layer, (SG_CHUNK, SG_GROUPS * SG_CHUNK)),
            _layer_spec(layer, (SG_CHUNK, D_B)),
            _layer_spec(layer, (1, D_B)),
            _layer_spec(layer, (1, LANES)),
            _layer_spec(layer, (1, LANES)),
            _layer_spec(layer, (1, D_B)),
        ],
        out_specs=[pl.BlockSpec((TM, w), row_map) for w in widths],
        out_shape=[jax.ShapeDtypeStruct((t_rows, w), BF16) for w in widths],
        scratch_shapes=[pltpu.VMEM((TM, D_IN), F32), pltpu.VMEM((TM, D_IN), F32)],
        compiler_params=pltpu.CompilerParams(
            dimension_semantics=("arbitrary",), vmem_limit_bytes=VMEM_LIMIT),
        name="inproj",
    )(x_all, mod, w_in_b, cos_t, sin_t, wsc, bsg, gsgu, gq2, gk2, gob)


def _split_heads(qg):
    lo = _low_half(qg.shape)
    zero = jnp.zeros_like(qg)
    return jnp.concatenate([jnp.where(lo, qg, zero), jnp.where(lo, zero, qg)], 0)


def _join_heads(o, m):
    return jnp.where(_low_half((m, LANES)), o[0:m], o[m:2 * m])


def _write_merged(o_ref, rows, outs, gout_ref):
    width = len(outs) * LANES
    ss = outs[0] * outs[0]
    for o in outs[1:]:
        ss = ss + o * o
    r = lax.rsqrt(jnp.sum(ss, -1, keepdims=True) * (1.0 / width) + LN_EPS)
    for p, o in enumerate(outs):
        cols = slice(p * LANES, (p + 1) * LANES)
        o_ref[rows, cols] = (o * r * gout_ref[:, cols]).astype(BF16)


def _na_kernel(q_ref, k_ref, v_ref, kx_ref, vx_ref, bias_ref, gout_ref, o_ref, *, grid_rows, n_row_blocks):
    t = pl.program_id(1)
    n_pairs = D_A // LANES
    span = NA_UNION_ROWS * GRID_W

    @pl.when(t < n_row_blocks)
    def _latent():
        first = jnp.clip(t * NA_ROWS_PER_STEP - NA_WIN_R // 2, 0, grid_rows - NA_UNION_ROWS)
        start = pl.multiple_of(first * GRID_W, GRID_W)
        outs = []
        for p in range(n_pairs):
            cols = slice(p * LANES, (p + 1) * LANES)
            lhs = _split_heads(q_ref[:, cols])
            s_loc = _dot_nt(lhs, k_ref[pl.ds(start, span), cols]) + bias_ref[0, p]
            s_ctx = _dot_nt(lhs, kx_ref[:, cols])
            m = jnp.maximum(jnp.max(s_loc, -1, keepdims=True), jnp.max(s_ctx, -1, keepdims=True))
            e_loc = jnp.exp(s_loc - m)
            e_ctx = jnp.exp(s_ctx - m)
            l = jnp.sum(e_loc, -1, keepdims=True) + jnp.sum(e_ctx, -1, keepdims=True)
            o = _dot(e_loc.astype(BF16), v_ref[pl.ds(start, span), cols]) \
                + _dot(e_ctx.astype(BF16), vx_ref[:, cols])
            outs.append(_join_heads(o * (1.0 / l), TQ))
        _write_merged(o_ref, slice(0, TQ), outs, gout_ref)

    @pl.when(t == n_row_blocks)
    def _context():
        outs = []
        for p in range(n_pairs):
            cols = slice(p * LANES, (p + 1) * LANES)
            lhs = _split_heads(q_ref[:, cols])
            s = _dot_nt(lhs, kx_ref[:, cols])
            e = jnp.exp(s - jnp.max(s, -1, keepdims=True))
            o = _dot(e.astype(BF16), vx_ref[:, cols]) * (1.0 / jnp.sum(e, -1, keepdims=True))
            outs.append(_join_heads(o, TQ))
        _write_merged(o_ref, slice(0, TQ), outs, gout_ref)


def _na(qa, ka, va, bias, gout_a, *, layer, n_batch, seq, ctx_len, with_ctx):
    grid_rows = seq // GRID_W
    n_row_blocks = seq // TQ
    assert grid_rows >= NA_UNION_ROWS + NA_ROWS_PER_STEP
    ctx_block0 = n_batch * seq // ctx_len
    q_ctx_block0 = n_batch * seq // TQ
    out_rows = qa.shape[0] if with_ctx else n_batch * seq

    def q_map(b, t):
        return (jnp.where(t < n_row_blocks, b * n_row_blocks + t, q_ctx_block0 + b), 0)

    def bias_map(b, t):
        return (layer, (t > 0).astype(jnp.int32) + (t >= n_row_blocks - 1).astype(jnp.int32), 0, 0, 0)

    return pl.pallas_call(
        functools.partial(_na_kernel, grid_rows=grid_rows, n_row_blocks=n_row_blocks),
        grid=(n_batch, n_row_blocks + (1 if with_ctx else 0)),
        in_specs=[
            pl.BlockSpec((TQ, D_A), q_map),
            pl.BlockSpec((seq, D_A), lambda b, t: (b, 0)),
            pl.BlockSpec((seq, D_A), lambda b, t: (b, 0)),
            pl.BlockSpec((ctx_len, D_A), lambda b, t: (ctx_block0 + b, 0)),
            pl.BlockSpec((ctx_len, D_A), lambda b, t: (ctx_block0 + b, 0)),
            pl.BlockSpec((None, 1) + bias.shape[2:], bias_map),
            _layer_spec(layer, (1, D_A)),
        ],
        out_specs=pl.BlockSpec((TQ, D_A), q_map),
        out_shape=jax.ShapeDtypeStruct((out_rows, D_A), BF16),
        compiler_params=pltpu.CompilerParams(
            dimension_semantics=("arbitrary", "arbitrary"), vmem_limit_bytes=VMEM_LIMIT),
        name="na",
    )(qa, ka, va, ka, va, bias, gout_a)


def _gqa_kernel(q_ref, k_ref, v_ref, kx_ref, vx_ref, gq_ref, gk_ref, gout_ref, o_ref,
                lhs_ref, m_ref, l_ref, acc_ref, *, n_q_blocks, n_k_chunks):
    t = pl.program_id(1)
    n_groups = D_C // LANES
    m_rows = GQA_Q_HEADS * TQ

    for j in range(GQA_KV_HEADS):
        for g in range(n_groups):
            qg = q_ref[:, g * LANES:(g + 1) * LANES]
            lo = _low_half(qg.shape)
            keep = lo if j == 0 else jnp.logical_not(lo)
            lhs_ref[(j * n_groups + g) * TQ:(j * n_groups + g + 1) * TQ, :] = jnp.where(keep, qg, jnp.zeros_like(qg))

    score_bound = (HEAD_DIM * ATTN_SCALE * LOG2_E) * jnp.max(jnp.abs(gq_ref[...])) * jnp.max(jnp.abs(gk_ref[...]))
    bounded = score_bound <= SAFE_SCORE_BOUND

    @pl.when(bounded)
    def _fixed_shift():
        def chunk(k_c, v_c):
            p = jnp.exp2(_dot_nt(lhs_ref[...], k_c) - score_bound)
            l_part = p[:, 0:LANES]
            for i in range(1, k_c.shape[0] // LANES):
                l_part = l_part + p[:, i * LANES:(i + 1) * LANES]
            return l_part, _dot(p.astype(BF16), v_c)

        l0, a0 = chunk(kx_ref[...], vx_ref[...])
        l_ref[...] = l0
        acc_ref[...] = a0

        @pl.when(t < n_q_blocks)
        def _latent_keys():
            def body(c, carry):
                start = pl.multiple_of(c * TK, TK)
                l_c, a_c = chunk(k_ref[pl.ds(start, TK), :], v_ref[pl.ds(start, TK), :])
                l_ref[...] += l_c
                acc_ref[...] += a_c
                return carry

            lax.fori_loop(0, n_k_chunks, body, 0, unroll=GQA_UNROLL)

        acc_ref[...] = acc_ref[...] * (1.0 / jnp.sum(l_ref[...], -1, keepdims=True))

    @pl.when(jnp.logical_not(bounded))
    def _running_max():
        s = _dot_nt(lhs_ref[...], kx_ref[...])
        m0 = jnp.max(s, -1, keepdims=True)
        e = jnp.exp2(s - m0)
        m_ref[...] = jnp.broadcast_to(m0, (m_rows, LANES))
        l_ref[...] = jnp.broadcast_to(jnp.sum(e, -1, keepdims=True), (m_rows, LANES))
        acc_ref[...] = _dot(e.astype(BF16), vx_ref[...])

        @pl.when(t < n_q_blocks)
        def _latent_keys():
            def body(c, carry):
                start = pl.multiple_of(c * TK, TK)
                s = _dot_nt(lhs_ref[...], k_ref[pl.ds(start, TK), :])
                m_prev = m_ref[:, 0:1]
                m_new = jnp.maximum(m_prev, jnp.max(s, -1, keepdims=True))
                alpha = jnp.exp2(m_prev - m_new)
                e = jnp.exp2(s - m_new)
                l_new = alpha * l_ref[:, 0:1] + jnp.sum(e, -1, keepdims=True)
                acc_ref[...] = alpha * acc_ref[...] + _dot(e.astype(BF16), v_ref[pl.ds(start, TK), :])
                m_ref[...] = jnp.broadcast_to(m_new, (m_rows, LANES))
                l_ref[...] = jnp.broadcast_to(l_new, (m_rows, LANES))
                return carry

            lax.fori_loop(0, n_k_chunks, body, 0)

        acc_ref[...] = acc_ref[...] * (1.0 / l_ref[:, 0:1])

    lo = _low_half((TQ, LANES))
    outs = [jnp.where(lo, acc_ref[g * TQ:(g + 1) * TQ, :], acc_ref[(n_groups + g) * TQ:(n_groups + g + 1) * TQ, :])
            for g in range(n_groups)]
    _write_merged(o_ref, slice(0, TQ), outs, gout_ref)


def _gqa(qc, kc, vc, gq2, gk2, gout_c, *, layer, n_batch, seq, ctx_len, with_ctx):
    out_rows = qc.shape[0] if with_ctx else n_batch * seq
    n_q_blocks = seq // TQ
    ctx_block0 = n_batch * seq // ctx_len
    q_ctx_block0 = n_batch * seq // TQ
    m_rows = GQA_Q_HEADS * TQ

    def q_map(b, t):
        return (jnp.where(t < n_q_blocks, b * n_q_blocks + t, q_ctx_block0 + b), 0)

    return pl.pallas_call(
        functools.partial(_gqa_kernel, n_q_blocks=n_q_blocks, n_k_chunks=seq // TK),
        grid=(n_batch, n_q_blocks + (1 if with_ctx else 0)),
        in_specs=[
            pl.BlockSpec((TQ, D_C), q_map),
            pl.BlockSpec((seq, D_KV_C), lambda b, t: (b, 0)),
            pl.BlockSpec((seq, D_KV_C), lambda b, t: (b, 0)),
            pl.BlockSpec((ctx_len, D_KV_C), lambda b, t: (ctx_block0 + b, 0)),
            pl.BlockSpec((ctx_len, D_KV_C), lambda b, t: (ctx_block0 + b, 0)),
            _layer_spec(layer, (1, LANES)),
            _layer_spec(layer, (1, LANES)),
            _layer_spec(layer, (1, D_C)),
        ],
        out_specs=pl.BlockSpec((TQ, D_C), q_map),
        out_shape=jax.ShapeDtypeStruct((out_rows, D_C), BF16),
        scratch_shapes=[
            pltpu.VMEM((m_rows, LANES), BF16),
            pltpu.VMEM((m_rows, LANES), F32),
            pltpu.VMEM((m_rows, LANES), F32),
            pltpu.VMEM((m_rows, LANES), F32),
        ],
        compiler_params=pltpu.CompilerParams(
            dimension_semantics=("arbitrary", "arbitrary"), vmem_limit_bytes=VMEM_LIMIT),
        name="gqa",
    )(qc, kc, vc, kc, vc, gq2, gk2, gout_c)


def _post_kernel(oa_ref, ob_ref, oc_ref, x_ref, mod_ref, wo_ref, wffn_ref, wout_ref,
                 ln1g_ref, ln1b_ref, ln2g_ref, ln2b_ref, o_ref, h_ref, g_ref):
    def mod(i):
        return mod_ref[0, :, i * D_MODEL:(i + 1) * D_MODEL]

    y = (_dot(oa_ref[...], wo_ref[0:D_A, :]) + _dot(ob_ref[...], wo_ref[D_A:D_A + D_B, :])
         + _dot(oc_ref[...], wo_ref[D_A + D_B:, :]))
    x1 = _layer_norm(ALPHA * x_ref[...] + mod(2) * y) * ln1g_ref[...] + ln1b_ref[...]
    h_ref[...] = (_layer_norm(x1) * (1.0 + mod(4)) + mod(3)).astype(BF16)
    for c in range(D_FF // FF_CHUNK):
        cols = slice(c * FF_CHUNK, (c + 1) * FF_CHUNK)
        gate = _dot(h_ref[...], wffn_ref[:, cols])
        up = _dot(h_ref[...], wffn_ref[:, D_FF + c * FF_CHUNK:D_FF + (c + 1) * FF_CHUNK])
        g_ref[:, cols] = (_silu(gate) * up).astype(BF16)
    ff = _dot(g_ref[...], wout_ref[...])
    o_ref[...] = _layer_norm(ALPHA * x1 + mod(5) * ff) * ln2g_ref[...] + ln2b_ref[...]


def _post(oa, ob, oc, x_all, mod, wo, wffn, wout, ln1g, ln1b, ln2g, ln2b, *, layer, n_lat, lat_per_batch,
          n_batch, n_tiles):
    def row_map(t):
        return (t, 0)

    def const2(t):
        return (0, 0)

    def mod_map(t):
        return (layer, jnp.where(t < n_lat, t // lat_per_batch, n_batch), 0, 0)

    def resident(shape):
        return pl.BlockSpec((None,) + shape, lambda t: (layer, 0, 0), pipeline_mode=pl.Buffered(1))

    return pl.pallas_call(
        _post_kernel,
        grid=(n_tiles,),
        in_specs=[
            pl.BlockSpec((TM, D_A), row_map),
            pl.BlockSpec((TM, D_B), row_map),
            pl.BlockSpec((TM, D_C), row_map),
            pl.BlockSpec((TM, D_MODEL), row_map),
            pl.BlockSpec((None, 1, 1, N_MOD * D_MODEL), mod_map),
            resident((D_MODEL, D_MODEL)),
            resident((D_MODEL, 2 * D_FF)),
            resident((D_FF, D_MODEL)),
            _layer_spec(layer, (1, D_MODEL)),
            _layer_spec(layer, (1, D_MODEL)),
            _layer_spec(layer, (1, D_MODEL)),
            _layer_spec(layer, (1, D_MODEL)),
        ],
        out_specs=pl.BlockSpec((TM, D_MODEL), row_map),
        out_shape=jax.ShapeDtypeStruct((n_tiles * TM, D_MODEL), F32),
        scratch_shapes=[pltpu.VMEM((TM, D_MODEL), BF16), pltpu.VMEM((TM, D_FF), BF16)],
        compiler_params=pltpu.CompilerParams(
            dimension_semantics=("arbitrary",), vmem_limit_bytes=VMEM_LIMIT),
        name="post",
    )(oa, ob, oc, x_all, mod, wo, wffn, wout, ln1g, ln1b, ln2g, ln2b)


def _rope_tables(seq):
    t = jnp.arange(seq, dtype=jnp.int32)
    row = (t // GRID_W).astype(F32)
    col = (t % GRID_W).astype(F32)
    n_freq = HEAD_DIM // 4
    inv = 1.0 / (ROPE_THETA ** (jnp.arange(n_freq, dtype=F32) / n_freq))
    ang = jnp.stack([row[:, None] * inv, col[:, None] * inv], axis=1)
    cos = jnp.cos(ang)[:, :, None, :]
    sin = jnp.sin(ang)[:, :, None, :]
    cos_h = jnp.broadcast_to(cos, (seq, 2, 2, n_freq)).reshape(seq, HEAD_DIM)
    sign = jnp.array([-1.0, 1.0], F32)[None, None, :, None]
    sin_h = jnp.broadcast_to(sin * sign, (seq, 2, 2, n_freq)).reshape(seq, HEAD_DIM)
    cos_t = jnp.concatenate([jnp.tile(cos_h, (1, 2)), jnp.ones((TM, LANES), F32)], 0)
    sin_t = jnp.concatenate([jnp.tile(sin_h, (1, 2)), jnp.zeros((TM, LANES), F32)], 0)
    return cos_t, sin_t


def _na_bias_tables(rpb):
    depth = rpb.shape[0]
    c_idx = np.arange(GRID_W)
    col_start = np.clip(c_idx - NA_WIN_C // 2, 0, GRID_W - NA_WIN_C)
    kc = np.arange(GRID_W)
    col_ok = (kc[None, :] >= col_start[:, None]) & (kc[None, :] < col_start[:, None] + NA_WIN_C)
    pad = GRID_W - NA_WIN_C
    padded = jnp.pad(rpb, ((0, 0), (0, 0), (0, 0), (pad, pad)))
    toep = jnp.stack([padded[..., GRID_W - 1 - c:2 * GRID_W - 1 - c] for c in range(GRID_W)], axis=-2)
    toep = jnp.where(jnp.asarray(col_ok), toep, MASK_VALUE)
    lead = np.array([0, NA_WIN_R // 2, NA_UNION_ROWS - NA_ROWS_PER_STEP])[:, None, None]
    i = np.arange(NA_ROWS_PER_STEP)[None, :, None]
    j = np.arange(NA_UNION_ROWS)[None, None, :]
    win_first = np.stack([np.zeros((NA_ROWS_PER_STEP, 1), np.int64), np.arange(NA_ROWS_PER_STEP)[:, None],
                          np.full((NA_ROWS_PER_STEP, 1), NA_UNION_ROWS - NA_WIN_R)])
    row_ok = (j >= win_first) & (j < win_first + NA_WIN_R)
    dr = j - lead - i + (NA_WIN_R - 1)
    masked = jnp.full((depth, NA_HEADS, GRID_W, GRID_W), MASK_VALUE, F32)
    variants = []
    for v in range(3):
        per_row = []
        for qi in range(NA_ROWS_PER_STEP):
            blocks = [toep[:, :, int(dr[v, qi, kj])] if row_ok[v, qi, kj] else masked
                      for kj in range(NA_UNION_ROWS)]
            per_row.append(jnp.stack(blocks, axis=-2))
        variants.append(jnp.stack(per_row, axis=2))
    tab = jnp.stack(variants, axis=1)
    return tab.reshape(depth, 3, NA_HEADS // 2, 2 * TQ, NA_UNION_ROWS * GRID_W)


def kernel(x, c, ctx, c_ctx, w_mod, b_mod, w_in, rpb, w_s, b_s, g_sgu, g_q, g_k, g_out, w_o,
           ln1_g, ln1_b, w_ffn_in, w_ffn_out, ln2_g, ln2_b):
    n_batch, seq, _ = x.shape
    ctx_len = ctx.shape[1]
    depth = w_in.shape[0]
    assert seq % TM == 0 and (n_batch * ctx_len) % TM == 0 and ctx_len == TQ and seq % TK == 0
    assert n_batch + 1 <= 8
    n_lat = n_batch * seq // TM
    n_tiles = n_lat + n_batch * ctx_len // TM
    lat_per_batch = seq // TM

    x_all = jnp.concatenate([x.reshape(n_batch * seq, D_MODEL), ctx.reshape(n_batch * ctx_len, D_MODEL)], 0)

    c_all = jnp.zeros((8, D_MODEL), F32).at[:n_batch].set(c).at[n_batch].set(c_ctx)
    mod = _modulation(c_all, w_mod, b_mod)[:, :n_batch + 1].reshape(depth, n_batch + 1, 1, N_MOD * D_MODEL)

    def reorder_gqa_heads(a, axis, off):
        parts = [lax.slice_in_dim(a, 0, off, axis=axis)]
        parts += [lax.slice_in_dim(a, off + h * HEAD_DIM, off + (h + 1) * HEAD_DIM, axis=axis)
                  for h in GQA_HEAD_ORDER]
        parts.append(lax.slice_in_dim(a, off + D_C, a.shape[axis], axis=axis))
        return jnp.concatenate(parts, axis)

    def rows(v):
        return v.reshape(depth, 1, -1)

    w_in_b = reorder_gqa_heads(w_in, 2, OFF_QC).astype(BF16)
    w_o_b = reorder_gqa_heads(w_o, 1, D_A + D_B).astype(BF16)
    g_out_p = reorder_gqa_heads(g_out, 1, D_A + D_B)
    g_out_a, g_out_b, g_out_c = (rows(g_out_p[:, :D_A]), rows(g_out_p[:, D_A:D_A + D_B]),
                                 rows(g_out_p[:, D_A + D_B:]))
    w_ffn_b = w_ffn_in.astype(BF16)
    w_out_b = w_ffn_out.astype(BF16)
    w_s_cat = jnp.transpose(w_s, (0, 2, 1, 3)).reshape(depth, SG_CHUNK, SG_GROUPS * SG_CHUNK).astype(BF16)
    b_s_full = jnp.repeat(jnp.transpose(b_s, (0, 2, 1)), HEAD_DIM, axis=2)
    g_sgu3 = rows(g_sgu)
    g_q2 = rows(jnp.tile(g_q, (1, 2)))
    g_k2 = rows(jnp.tile(g_k, (1, 2)))
    ln1g, ln1b, ln2g, ln2b = rows(ln1_g), rows(ln1_b), rows(ln2_g), rows(ln2_b)
    cos_t, sin_t = _rope_tables(seq)
    bias = _na_bias_tables(rpb)

    for l in range(depth):
        with_ctx = l < depth - 1
        qa, ka, va, ob, qc, kc, vc = _inproj(
            x_all, mod, w_in_b, cos_t, sin_t, w_s_cat, b_s_full, g_sgu3, g_q2, g_k2, g_out_b,
            layer=l, n_lat=n_lat, lat_per_batch=lat_per_batch, n_batch=n_batch, rope_lat_tiles=seq // TM)
        oa = _na(qa, ka, va, bias, g_out_a, layer=l, n_batch=n_batch, seq=seq, ctx_len=ctx_len, with_ctx=with_ctx)
        oc = _gqa(qc, kc, vc, g_q2, g_k2, g_out_c,
                  layer=l, n_batch=n_batch, seq=seq, ctx_len=ctx_len, with_ctx=with_ctx)
        x_all = _post(oa, ob, oc, x_all, mod, w_o_b, w_ffn_b, w_out_b, ln1g, ln1b, ln2g, ln2b,
                      layer=l, n_lat=n_lat, lat_per_batch=lat_per_batch, n_batch=n_batch,
                      n_tiles=n_tiles if with_ctx else n_lat)
    return x_all.reshape(n_batch, seq, D_MODEL)
```

```python
import functools
import math

import jax
import jax.numpy as jnp
import numpy as np
from jax import lax
from jax.experimental import pallas as pl
from jax.experimental.pallas import tpu as pltpu

F32 = jnp.float32
BF16 = jnp.bfloat16

D_MODEL = 1024
HEAD_DIM = 64
GRID_W = 64
NA_HEADS = 6
NA_WIN_R = 8
NA_WIN_C = 16
SG_GROUPS = 4
SG_CHUNK = 128
GQA_Q_HEADS = 6
GQA_KV_HEADS = 2
ROPE_THETA = 10000.0
MODEL_DEPTH = 4

D_A = NA_HEADS * HEAD_DIM
D_B = SG_GROUPS * HEAD_DIM
D_C = GQA_Q_HEADS * HEAD_DIM
D_KV_C = GQA_KV_HEADS * HEAD_DIM
D_IN = 3 * D_A + 2 * D_B + D_C + 2 * D_KV_C
D_FF = int(math.ceil(8 * D_MODEL / 3 / 256)) * 256
N_MOD = 6
ALPHA = (2 * MODEL_DEPTH) ** 0.25
LN_EPS = 1e-6
ATTN_SCALE = HEAD_DIM ** -0.5
MASK_VALUE = -1e30
LOG2_E = 1.4426950408889634
SAFE_SCORE_BOUND = 57.0

OFF_QA, OFF_KA, OFF_VA = 0, D_A, 2 * D_A
OFF_ZB = 3 * D_A
OFF_QC = OFF_ZB + 2 * D_B
OFF_KC = OFF_QC + D_C
OFF_VC = OFF_KC + D_KV_C

LANES = 128
TM = 512
TQ = 256
TK = 512
GQA_UNROLL = 16
NA_ROWS_PER_STEP = TQ // GRID_W
NA_UNION_ROWS = 12
FF_CHUNK = 256
MOD_TN = 1536
VMEM_LIMIT = 56 * 1024 * 1024

GQA_HEAD_ORDER = (0, 3, 1, 4, 2, 5)


def _dot(a, b):
    return jnp.dot(a, b, preferred_element_type=F32)


def _dot_nt(a, b):
    return lax.dot_general(a, b, (((1,), (1,)), ((), ())), preferred_element_type=F32)


def _layer_norm(x):
    mu = jnp.mean(x, -1, keepdims=True)
    xc = x - mu
    var = jnp.mean(xc * xc, -1, keepdims=True)
    return xc * lax.rsqrt(var + LN_EPS)


def _rms(x):
    return x * lax.rsqrt(jnp.mean(x * x, -1, keepdims=True) + LN_EPS)


def _gelu_tanh(x):
    return x * (0.5 * (1.0 + jnp.tanh(0.7978845608028654 * (x + 0.044715 * (x * x * x)))))


def _silu(x):
    return x * (1.0 / (1.0 + jnp.exp(-x)))


def _low_half(shape):
    return lax.broadcasted_iota(jnp.int32, shape, 1) < HEAD_DIM


def _pair_rms(xg, gain):
    lo = _low_half(xg.shape)
    sq = xg * xg
    s_lo = jnp.sum(jnp.where(lo, sq, 0.0), -1, keepdims=True)
    s_hi = jnp.sum(jnp.where(lo, 0.0, sq), -1, keepdims=True)
    r = jnp.where(lo, lax.rsqrt(s_lo * (1.0 / HEAD_DIM) + LN_EPS),
                  lax.rsqrt(s_hi * (1.0 / HEAD_DIM) + LN_EPS))
    return xg * r * gain


def _rope(xn, cos_t, sin_t):
    lane = lax.broadcasted_iota(jnp.int32, xn.shape, 1)
    first = (lane % 32) < 16
    partner = jnp.where(first, pltpu.roll(xn, LANES - 16, 1), pltpu.roll(xn, 16, 1))
    return xn * cos_t + partner * sin_t


def _mod_kernel(c_ref, w_ref, b_ref, o_ref):
    sc = _silu(c_ref[...]).astype(BF16)
    o_ref[...] = _dot(sc, w_ref[...].astype(BF16)) + b_ref[...]


def _modulation(c_all, w_mod, b_mod):
    depth = w_mod.shape[0]
    n_out = w_mod.shape[2]
    return pl.pallas_call(
        _mod_kernel,
        grid=(depth, n_out // MOD_TN),
        in_specs=[
            pl.BlockSpec((8, D_MODEL), lambda l, j: (0, 0)),
            pl.BlockSpec((None, D_MODEL, MOD_TN), lambda l, j: (l, 0, j)),
            pl.BlockSpec((None, 1, MOD_TN), lambda l, j: (l, 0, j)),
        ],
        out_specs=pl.BlockSpec((None, 8, MOD_TN), lambda l, j: (l, 0, j)),
        out_shape=jax.ShapeDtypeStruct((depth, 8, n_out), F32),
        compiler_params=pltpu.CompilerParams(
            dimension_semantics=("arbitrary", "arbitrary"), vmem_limit_bytes=VMEM_LIMIT),
        name="modulation",
    )(c_all, w_mod, b_mod.reshape(depth, 1, n_out))


def _inproj_kernel(x_ref, mod_ref, w_ref, cos_ref, sin_ref, wsc_ref, bsg_ref, gsgu_ref,
                   gq_ref, gk_ref, gob_ref,
                   qa_ref, ka_ref, va_ref, ob_ref, qc_ref, kc_ref, vc_ref, y_even_ref, y_odd_ref):
    t = pl.program_id(0)

    @pl.when(t == 0)
    def _no_previous_tile():
        y_odd_ref[...] = jnp.zeros_like(y_odd_ref)

    @pl.when(t % 2 == 0)
    def _even():
        _inproj_step(x_ref, mod_ref, w_ref, cos_ref, sin_ref, wsc_ref, bsg_ref, gsgu_ref, gq_ref, gk_ref, gob_ref,
                     qa_ref, ka_ref, va_ref, ob_ref, qc_ref, kc_ref, vc_ref, y_even_ref, y_odd_ref)

    @pl.when(t % 2 == 1)
    def _odd():
        _inproj_step(x_ref, mod_ref, w_ref, cos_ref, sin_ref, wsc_ref, bsg_ref, gsgu_ref, gq_ref, gk_ref, gob_ref,
                     qa_ref, ka_ref, va_ref, ob_ref, qc_ref, kc_ref, vc_ref, y_odd_ref, y_even_ref)


def _inproj_step(x_ref, mod_ref, w_ref, cos_ref, sin_ref, wsc_ref, bsg_ref, gsgu_ref, gq_ref, gk_ref, gob_ref,
                 qa_ref, ka_ref, va_ref, ob_ref, qc_ref, kc_ref, vc_ref, y_new_ref, y_ref):
    shift = mod_ref[0, :, 0:D_MODEL]
    scale = mod_ref[0, :, D_MODEL:2 * D_MODEL]
    h = (_layer_norm(x_ref[...]) * (1.0 + scale) + shift).astype(BF16)
    y_new_ref[...] = _dot(h, w_ref[...])

    def proj(off, width):
        return y_ref[:, off:off + width]

    qa_ref[...] = (proj(OFF_QA, D_A) * ATTN_SCALE).astype(BF16)
    ka_ref[...] = proj(OFF_KA, D_A).astype(BF16)
    va_ref[...] = proj(OFF_VA, D_A).astype(BF16)

    u = _gelu_tanh(proj(OFF_ZB, D_B))
    v = (_layer_norm(_gelu_tanh(proj(OFF_ZB + D_B, D_B))) * gsgu_ref[...]).astype(BF16)
    lane_group = lax.broadcasted_iota(jnp.int32, (SG_CHUNK, D_B), 1) // HEAD_DIM
    for c in range(TM // SG_CHUNK):
        rows = slice(c * SG_CHUNK, (c + 1) * SG_CHUNK)
        vch = v[rows]
        rhs = jnp.concatenate(
            [jnp.where(lane_group == g, vch, jnp.zeros_like(vch)) for g in range(SG_GROUPS)], 0)
        mixed = _dot(wsc_ref[...], rhs) + bsg_ref[...]
        ob_ref[rows, :] = (_rms(u[rows] * mixed) * gob_ref[...]).astype(BF16)

    cos_t = cos_ref[...]
    sin_t = sin_ref[...]
    for p in range(D_C // LANES):
        xq = proj(OFF_QC + p * LANES, LANES)
        xq = _rope(_pair_rms(xq, gq_ref[...]), cos_t, sin_t)
        qc_ref[:, p * LANES:(p + 1) * LANES] = (xq * (ATTN_SCALE * LOG2_E)).astype(BF16)
    xk = _rope(_pair_rms(proj(OFF_KC, D_KV_C), gk_ref[...]), cos_t, sin_t)
    kc_ref[...] = xk.astype(BF16)
    vc_ref[...] = proj(OFF_VC, D_KV_C).astype(BF16)


def _layer_spec(layer, shape):
    zeros = (0,) * len(shape)
    return pl.BlockSpec((None,) + tuple(shape), lambda *_: (layer,) + zeros)


def _inproj(x_all, mod, w_in_b, cos_t, sin_t, wsc, bsg, gsgu, gq2, gk2, gob, *, layer, n_lat, lat_per_batch,
            n_batch, rope_lat_tiles):
    t_rows = x_all.shape[0]
    n_tiles = t_rows // TM

    def in_map(t):
        return (jnp.minimum(t, n_tiles - 1), 0)

    def row_map(t):
        return (jnp.maximum(t - 1, 0), 0)

    def mod_map(t):
        tile = jnp.minimum(t, n_tiles - 1)
        return (layer, jnp.where(tile < n_lat, tile // lat_per_batch, n_batch), 0, 0)

    def rope_map(t):
        tile = jnp.maximum(t - 1, 0)
        return (jnp.where(tile < n_lat, tile % rope_lat_tiles, rope_lat_tiles), 0)

    widths = (D_A, D_A, D_A, D_B, D_C, D_KV_C, D_KV_C)
    return pl.pallas_call(
        _inproj_kernel,
        grid=(n_tiles + 1,),
        in_specs=[
            pl.BlockSpec((TM, D_MODEL), in_map),
            pl.BlockSpec((None, 1, 1, N_MOD * D_MODEL), mod_map),
            _layer_spec(layer, (D_MODEL, D_IN)),
            pl.BlockSpec((TM, LANES), rope_map),
            pl.BlockSpec((TM, LANES), rope_map),
            _layer_spec(layer, (SG_CHUNK, SG_GROUPS * SG_CHUNK)),
            _layer_spec(layer, (SG_CHUNK, D_B)),
            _layer_spec(layer, (1, D_B)),
            _layer_spec(layer, (1, LANES)),
            _layer_spec(layer, (1, LANES)),
            _layer_spec(layer, (1, D_B)),
        ],
        out_specs=[pl.BlockSpec((TM, w), row_map) for w in widths],
        out_shape=[jax.ShapeDtypeStruct((t_rows, w), BF16) for w in widths],
        scratch_shapes=[pltpu.VMEM((TM, D_IN), F32), pltpu.VMEM((TM, D_IN), F32)],
        compiler_params=pltpu.CompilerParams(
            dimension_semantics=("arbitrary",), vmem_limit_bytes=VMEM_LIMIT),
        name="inproj",
    )(x_all, mod, w_in_b, cos_t, sin_t, wsc, bsg, gsgu, gq2, gk2, gob)


def _split_heads(qg):
    lo = _low_half(qg.shape)
    zero = jnp.zeros_like(qg)
    return jnp.concatenate([jnp.where(lo, qg, zero), jnp.where(lo, zero, qg)], 0)


def _join_heads(o, m):
    return jnp.where(_low_half((m, LANES)), o[0:m], o[m:2 * m])


def _write_merged(o_ref, rows, outs, gout_ref):
    width = len(outs) * LANES
    ss = outs[0] * outs[0]
    for o in outs[1:]:
        ss = ss + o * o
    r = lax.rsqrt(jnp.sum(ss, -1, keepdims=True) * (1.0 / width) + LN_EPS)
    for p, o in enumerate(outs):
        cols = slice(p * LANES, (p + 1) * LANES)
        o_ref[rows, cols] = (o * r * gout_ref[:, cols]).astype(BF16)


def _na_kernel(q_ref, k_ref, v_ref, kx_ref, vx_ref, bias_ref, gout_ref, o_ref, *, grid_rows, n_row_blocks):
    t = pl.program_id(1)
    n_pairs = D_A // LANES
    span = NA_UNION_ROWS * GRID_W

    @pl.when(t < n_row_blocks)
    def _latent():
        first = jnp.clip(t * NA_ROWS_PER_STEP - NA_WIN_R // 2, 0, grid_rows - NA_UNION_ROWS)
        start = pl.multiple_of(first * GRID_W, GRID_W)
        outs = []
        for p in range(n_pairs):
            cols = slice(p * LANES, (p + 1) * LANES)
            lhs = _split_heads(q_ref[:, cols])
            s_loc = _dot_nt(lhs, k_ref[pl.ds(start, span), cols]) + bias_ref[0, p]
            s_ctx = _dot_nt(lhs, kx_ref[:, cols])
            m = jnp.maximum(jnp.max(s_loc, -1, keepdims=True), jnp.max(s_ctx, -1, keepdims=True))
            e_loc = jnp.exp(s_loc - m)
            e_ctx = jnp.exp(s_ctx - m)
            l = jnp.sum(e_loc, -1, keepdims=True) + jnp.sum(e_ctx, -1, keepdims=True)
            o = _dot(e_loc.astype(BF16), v_ref[pl.ds(start, span), cols]) \
                + _dot(e_ctx.astype(BF16), vx_ref[:, cols])
            outs.append(_join_heads(o * (1.0 / l), TQ))
        _write_merged(o_ref, slice(0, TQ), outs, gout_ref)

    @pl.when(t == n_row_blocks)
    def _context():
        outs = []
        for p in range(n_pairs):
            cols = slice(p * LANES, (p + 1) * LANES)
            lhs = _split_heads(q_ref[:, cols])
            s = _dot_nt(lhs, kx_ref[:, cols])
            e = jnp.exp(s - jnp.max(s, -1, keepdims=True))
            o = _dot(e.astype(BF16), vx_ref[:, cols]) * (1.0 / jnp.sum(e, -1, keepdims=True))
            outs.append(_join_heads(o, TQ))
        _write_merged(o_ref, slice(0, TQ), outs, gout_ref)


def _na(qa, ka, va, bias, gout_a, *, layer, n_batch, seq, ctx_len, with_ctx):
    grid_rows = seq // GRID_W
    n_row_blocks = seq // TQ
    assert grid_rows >= NA_UNION_ROWS + NA_ROWS_PER_STEP
    ctx_block0 = n_batch * seq // ctx_len
    q_ctx_block0 = n_batch * seq // TQ
    out_rows = qa.shape[0] if with_ctx else n_batch * seq

    def q_map(b, t):
        return (jnp.where(t < n_row_blocks, b * n_row_blocks + t, q_ctx_block0 + b), 0)

    def bias_map(b, t):
        return (layer, (t > 0).astype(jnp.int32) + (t >= n_row_blocks - 1).astype(jnp.int32), 0, 0, 0)

    return pl.pallas_call(
        functools.partial(_na_kernel, grid_rows=grid_rows, n_row_blocks=n_row_blocks),
        grid=(n_batch, n_row_blocks + (1 if with_ctx else 0)),
        in_specs=[
            pl.BlockSpec((TQ, D_A), q_map),
            pl.BlockSpec((seq, D_A), lambda b, t: (b, 0)),
            pl.BlockSpec((seq, D_A), lambda b, t: (b, 0)),
            pl.BlockSpec((ctx_len, D_A), lambda b, t: (ctx_block0 + b, 0)),
            pl.BlockSpec((ctx_len, D_A), lambda b, t: (ctx_block0 + b, 0)),
            pl.BlockSpec((None, 1) + bias.shape[2:], bias_map),
            _layer_spec(layer, (1, D_A)),
        ],
        out_specs=pl.BlockSpec((TQ, D_A), q_map),
        out_shape=jax.ShapeDtypeStruct((out_rows, D_A), BF16),
        compiler_params=pltpu.CompilerParams(
            dimension_semantics=("arbitrary", "arbitrary"), vmem_limit_bytes=VMEM_LIMIT),
        name="na",
    )(qa, ka, va, ka, va, bias, gout_a)


def _gqa_kernel(q_ref, k_ref, v_ref, kx_ref, vx_ref, gq_ref, gk_ref, gout_ref, o_ref,
                lhs_ref, m_ref, l_ref, acc_ref, *, n_q_blocks, n_k_chunks):
    t = pl.program_id(1)
    n_groups = D_C // LANES
    m_rows = GQA_Q_HEADS * TQ

    for j in range(GQA_KV_HEADS):
        for g in range(n_groups):
            qg = q_ref[:, g * LANES:(g + 1) * LANES]
            lo = _low_half(qg.shape)
            keep = lo if j == 0 else jnp.logical_not(lo)
            lhs_ref[(j * n_groups + g) * TQ:(j * n_groups + g + 1) * TQ, :] = jnp.where(keep, qg, jnp.zeros_like(qg))

    score_bound = (HEAD_DIM * ATTN_SCALE * LOG2_E) * jnp.max(jnp.abs(gq_ref[...])) * jnp.max(jnp.abs(gk_ref[...]))
    bounded = score_bound <= SAFE_SCORE_BOUND

    @pl.when(bounded)
    def _fixed_shift():
        def chunk(k_c, v_c):
            p = jnp.exp2(_dot_nt(lhs_ref[...], k_c) - score_bound)
            l_part = p[:, 0:LANES]
            for i in range(1, k_c.shape[0] // LANES):
                l_part = l_part + p[:, i * LANES:(i + 1) * LANES]
            return l_part, _dot(p.astype(BF16), v_c)

        l0, a0 = chunk(kx_ref[...], vx_ref[...])
        l_ref[...] = l0
        acc_ref[...] = a0

        @pl.when(t < n_q_blocks)
        def _latent_keys():
            def body(c, carry):
                start = pl.multiple_of(c * TK, TK)
                l_c, a_c = chunk(k_ref[pl.ds(start, TK), :], v_ref[pl.ds(start, TK), :])
                l_ref[...] += l_c
                acc_ref[...] += a_c
                return carry

            lax.fori_loop(0, n_k_chunks, body, 0, unroll=GQA_UNROLL)

        acc_ref[...] = acc_ref[...] * (1.0 / jnp.sum(l_ref[...], -1, keepdims=True))

    @pl.when(jnp.logical_not(bounded))
    def _running_max():
        s = _dot_nt(lhs_ref[...], kx_ref[...])
        m0 = jnp.max(s, -1, keepdims=True)
        e = jnp.exp2(s - m0)
        m_ref[...] = jnp.broadcast_to(m0, (m_rows, LANES))
        l_ref[...] = jnp.broadcast_to(jnp.sum(e, -1, keepdims=True), (m_rows, LANES))
        acc_ref[...] = _dot(e.astype(BF16), vx_ref[...])

        @pl.when(t < n_q_blocks)
        def _latent_keys():
            def body(c, carry):
                start = pl.multiple_of(c * TK, TK)
                s = _dot_nt(lhs_ref[...], k_ref[pl.ds(start, TK), :])
                m_prev = m_ref[:, 0:1]
                m_new = jnp.maximum(m_prev, jnp.max(s, -1, keepdims=True))
                alpha = jnp.exp2(m_prev - m_new)
                e = jnp.exp2(s - m_new)
                l_new = alpha * l_ref[:, 0:1] + jnp.sum(e, -1, keepdims=True)
                acc_ref[...] = alpha * acc_ref[...] + _dot(e.astype(BF16), v_ref[pl.ds(start, TK), :])
                m_ref[...] = jnp.broadcast_to(m_new, (m_rows, LANES))
                l_ref[...] = jnp.broadcast_to(l_new, (m_rows, LANES))
                return carry

            lax.fori_loop(0, n_k_chunks, body, 0)

        acc_ref[...] = acc_ref[...] * (1.0 / l_ref[:, 0:1])

    lo = _low_half((TQ, LANES))
    outs = [jnp.where(lo, acc_ref[g * TQ:(g + 1) * TQ, :], acc_ref[(n_groups + g) * TQ:(n_groups + g + 1) * TQ, :])
            for g in range(n_groups)]
    _write_merged(o_ref, slice(0, TQ), outs, gout_ref)


def _gqa(qc, kc, vc, gq2, gk2, gout_c, *, layer, n_batch, seq, ctx_len, with_ctx):
    out_rows = qc.shape[0] if with_ctx else n_batch * seq
    n_q_blocks = seq // TQ
    ctx_block0 = n_batch * seq // ctx_len
    q_ctx_block0 = n_batch * seq // TQ
    m_rows = GQA_Q_HEADS * TQ

    def q_map(b, t):
        return (jnp.where(t < n_q_blocks, b * n_q_blocks + t, q_ctx_block0 + b), 0)

    return pl.pallas_call(
        functools.partial(_gqa_kernel, n_q_blocks=n_q_blocks, n_k_chunks=seq // TK),
        grid=(n_batch, n_q_blocks + (1 if with_ctx else 0)),
        in_specs=[
            pl.BlockSpec((TQ, D_C), q_map),
            pl.BlockSpec((seq, D_KV_C), lambda b, t: (b, 0)),
            pl.BlockSpec((seq, D_KV_C), lambda b, t: (b, 0)),
            pl.BlockSpec((ctx_len, D_KV_C), lambda b, t: (ctx_block0 + b, 0)),
            pl.BlockSpec((ctx_len, D_KV_C), lambda b, t: (ctx_block0 + b, 0)),
            _layer_spec(layer, (1, LANES)),
            _layer_spec(layer, (1, LANES)),
            _layer_spec(layer, (1, D_C)),
        ],
        out_specs=pl.BlockSpec((TQ, D_C), q_map),
        out_shape=jax.ShapeDtypeStruct((out_rows, D_C), BF16),
        scratch_shapes=[
            pltpu.VMEM((m_rows, LANES), BF16),
            pltpu.VMEM((m_rows, LANES), F32),
            pltpu.VMEM((m_rows, LANES), F32),
            pltpu.VMEM((m_rows, LANES), F32),
        ],
        compiler_params=pltpu.CompilerParams(
            dimension_semantics=("arbitrary", "arbitrary"), vmem_limit_bytes=VMEM_LIMIT),
        name="gqa",
    )(qc, kc, vc, kc, vc, gq2, gk2, gout_c)


def _post_kernel(oa_ref, ob_ref, oc_ref, x_ref, mod_ref, wo_ref, wffn_ref, wout_ref,
                 ln1g_ref, ln1b_ref, ln2g_ref, ln2b_ref, o_ref, h_ref, g_ref):
    def mod(i):
        return mod_ref[0, :, i * D_MODEL:(i + 1) * D_MODEL]

    y = (_dot(oa_ref[...], wo_ref[0:D_A, :]) + _dot(ob_ref[...], wo_ref[D_A:D_A + D_B, :])
         + _dot(oc_ref[...], wo_ref[D_A + D_B:, :]))
    x1 = _layer_norm(ALPHA * x_ref[...] + mod(2) * y) * ln1g_ref[...] + ln1b_ref[...]
    h_ref[...] = (_layer_norm(x1) * (1.0 + mod(4)) + mod(3)).astype(BF16)
    for c in range(D_FF // FF_CHUNK):
        cols = slice(c * FF_CHUNK, (c + 1) * FF_CHUNK)
        gate = _dot(h_ref[...], wffn_ref[:, cols])
        up = _dot(h_ref[...], wffn_ref[:, D_FF + c * FF_CHUNK:D_FF + (c + 1) * FF_CHUNK])
        g_ref[:, cols] = (_silu(gate) * up).astype(BF16)
    ff = _dot(g_ref[...], wout_ref[...])
    o_ref[...] = _layer_norm(ALPHA * x1 + mod(5) * ff) * ln2g_ref[...] + ln2b_ref[...]


def _post(oa, ob, oc, x_all, mod, wo, wffn, wout, ln1g, ln1b, ln2g, ln2b, *, layer, n_lat, lat_per_batch,
          n_batch, n_tiles):
    def row_map(t):
        return (t, 0)

    def mod_map(t):
        return (layer, jnp.where(t < n_lat, t // lat_per_batch, n_batch), 0, 0)

    def resident(shape):
        return pl.BlockSpec((None,) + shape, lambda t: (layer, 0, 0), pipeline_mode=pl.Buffered(1))

    return pl.pallas_call(
        _post_kernel,
        grid=(n_tiles,),
        in_specs=[
            pl.BlockSpec((TM, D_A), row_map),
            pl.BlockSpec((TM, D_B), row_map),
            pl.BlockSpec((TM, D_C), row_map),
            pl.BlockSpec((TM, D_MODEL), row_map),
            pl.BlockSpec((None, 1, 1, N_MOD * D_MODEL), mod_map),
            resident((D_MODEL, D_MODEL)),
            resident((D_MODEL, 2 * D_FF)),
            resident((D_FF, D_MODEL)),
            _layer_spec(layer, (1, D_MODEL)),
            _layer_spec(layer, (1, D_MODEL)),
            _layer_spec(layer, (1, D_MODEL)),
            _layer_spec(layer, (1, D_MODEL)),
        ],
        out_specs=pl.BlockSpec((TM, D_MODEL), row_map),
        out_shape=jax.ShapeDtypeStruct((n_tiles * TM, D_MODEL), F32),
        scratch_shapes=[pltpu.VMEM((TM, D_MODEL), BF16), pltpu.VMEM((TM, D_FF), BF16)],
        compiler_params=pltpu.CompilerParams(
            dimension_semantics=("arbitrary",), vmem_limit_bytes=VMEM_LIMIT),
        name="post",
    )(oa, ob, oc, x_all, mod, wo, wffn, wout, ln1g, ln1b, ln2g, ln2b)


def _rope_tables(seq):
    t = jnp.arange(seq, dtype=jnp.int32)
    row = (t // GRID_W).astype(F32)
    col = (t % GRID_W).astype(F32)
    n_freq = HEAD_DIM // 4
    inv = 1.0 / (ROPE_THETA ** (jnp.arange(n_freq, dtype=F32) / n_freq))
    ang = jnp.stack([row[:, None] * inv, col[:, None] * inv], axis=1)
    cos = jnp.cos(ang)[:, :, None, :]
    sin = jnp.sin(ang)[:, :, None, :]
    cos_h = jnp.broadcast_to(cos, (seq, 2, 2, n_freq)).reshape(seq, HEAD_DIM)
    sign = jnp.array([-1.0, 1.0], F32)[None, None, :, None]
    sin_h = jnp.broadcast_to(sin * sign, (seq, 2, 2, n_freq)).reshape(seq, HEAD_DIM)
    cos_t = jnp.concatenate([jnp.tile(cos_h, (1, 2)), jnp.ones((TM, LANES), F32)], 0)
    sin_t = jnp.concatenate([jnp.tile(sin_h, (1, 2)), jnp.zeros((TM, LANES), F32)], 0)
    return cos_t, sin_t


def _na_bias_tables(rpb):
    depth = rpb.shape[0]
    c_idx = np.arange(GRID_W)
    col_start = np.clip(c_idx - NA_WIN_C // 2, 0, GRID_W - NA_WIN_C)
    kc = np.arange(GRID_W)
    col_ok = (kc[None, :] >= col_start[:, None]) & (kc[None, :] < col_start[:, None] + NA_WIN_C)
    pad = GRID_W - NA_WIN_C
    padded = jnp.pad(rpb, ((0, 0), (0, 0), (0, 0), (pad, pad)))
    toep = jnp.stack([padded[..., GRID_W - 1 - c:2 * GRID_W - 1 - c] for c in range(GRID_W)], axis=-2)
    toep = jnp.where(jnp.asarray(col_ok), toep, MASK_VALUE)
    lead = np.array([0, NA_WIN_R // 2, NA_UNION_ROWS - NA_ROWS_PER_STEP])[:, None, None]
    i = np.arange(NA_ROWS_PER_STEP)[None, :, None]
    j = np.arange(NA_UNION_ROWS)[None, None, :]
    win_first = np.stack([np.zeros((NA_ROWS_PER_STEP, 1), np.int64), np.arange(NA_ROWS_PER_STEP)[:, None],
                          np.full((NA_ROWS_PER_STEP, 1), NA_UNION_ROWS - NA_WIN_R)])
    row_ok = (j >= win_first) & (j < win_first + NA_WIN_R)
    dr = j - lead - i + (NA_WIN_R - 1)
    masked = jnp.full((depth, NA_HEADS, GRID_W, GRID_W), MASK_VALUE, F32)
    variants = []
    for v in range(3):
        per_row = []
        for qi in range(NA_ROWS_PER_STEP):
            blocks = [toep[:, :, int(dr[v, qi, kj])] if row_ok[v, qi, kj] else masked
                      for kj in range(NA_UNION_ROWS)]
            per_row.append(jnp.concatenate(blocks, axis=-1))
        variants.append(jnp.concatenate(per_row, axis=-2))
    tab = jnp.stack(variants, axis=1)
    return tab.reshape(depth, 3, NA_HEADS // 2, 2 * TQ, NA_UNION_ROWS * GRID_W)


def kernel(x, c, ctx, c_ctx, w_mod, b_mod, w_in, rpb, w_s, b_s, g_sgu, g_q, g_k, g_out, w_o,
           ln1_g, ln1_b, w_ffn_in, w_ffn_out, ln2_g, ln2_b):
    n_batch, seq, _ = x.shape
    ctx_len = ctx.shape[1]
    depth = w_in.shape[0]
    assert seq % TM == 0 and (n_batch * ctx_len) % TM == 0 and ctx_len == TQ and seq % TK == 0
    assert n_batch + 1 <= 8
    n_lat = n_batch * seq // TM
    n_tiles = n_lat + n_batch * ctx_len // TM
    lat_per_batch = seq // TM

    x_all = jnp.concatenate([x.reshape(n_batch * seq, D_MODEL), ctx.reshape(n_batch * ctx_len, D_MODEL)], 0)

    c_all = jnp.zeros((8, D_MODEL), F32).at[:n_batch].set(c).at[n_batch].set(c_ctx)
    mod = _modulation(c_all, w_mod, b_mod)[:, :n_batch + 1].reshape(depth, n_batch + 1, 1, N_MOD * D_MODEL)

    def reorder_gqa_heads(a, axis, off):
        parts = [lax.slice_in_dim(a, 0, off, axis=axis)]
        parts += [lax.slice_in_dim(a, off + h * HEAD_DIM, off + (h + 1) * HEAD_DIM, axis=axis)
                  for h in GQA_HEAD_ORDER]
        parts.append(lax.slice_in_dim(a, off + D_C, a.shape[axis], axis=axis))
        return jnp.concatenate(parts, axis)

    def rows(v):
        return v.reshape(depth, 1, -1)

    w_in_b = reorder_gqa_heads(w_in, 2, OFF_QC).astype(BF16)
    w_o_b = reorder_gqa_heads(w_o, 1, D_A + D_B).astype(BF16)
    g_out_p = reorder_gqa_heads(g_out, 1, D_A + D_B)
    g_out_a, g_out_b, g_out_c = (rows(g_out_p[:, :D_A]), rows(g_out_p[:, D_A:D_A + D_B]),
                                 rows(g_out_p[:, D_A + D_B:]))
    w_ffn_b = w_ffn_in.astype(BF16)
    w_out_b = w_ffn_out.astype(BF16)
    w_s_cat = jnp.transpose(w_s, (0, 2, 1, 3)).reshape(depth, SG_CHUNK, SG_GROUPS * SG_CHUNK).astype(BF16)
    b_s_full = jnp.repeat(jnp.transpose(b_s, (0, 2, 1)), HEAD_DIM, axis=2)
    g_sgu3 = rows(g_sgu)
    g_q2 = rows(jnp.tile(g_q, (1, 2)))
    g_k2 = rows(jnp.tile(g_k, (1, 2)))
    ln1g, ln1b, ln2g, ln2b = rows(ln1_g), rows(ln1_b), rows(ln2_g), rows(ln2_b)
    cos_t, sin_t = _rope_tables(seq)
    bias = _na_bias_tables(rpb)

    for l in range(depth):
        with_ctx = l < depth - 1
        qa, ka, va, ob, qc, kc, vc = _inproj(
            x_all, mod, w_in_b, cos_t, sin_t, w_s_cat, b_s_full, g_sgu3, g_q2, g_k2, g_out_b,
            layer=l, n_lat=n_lat, lat_per_batch=lat_per_batch, n_batch=n_batch, rope_lat_tiles=seq // TM)
        oa = _na(qa, ka, va, bias, g_out_a, layer=l, n_batch=n_batch, seq=seq, ctx_len=ctx_len, with_ctx=with_ctx)
        oc = _gqa(qc, kc, vc, g_q2, g_k2, g_out_c,
                  layer=l, n_batch=n_batch, seq=seq, ctx_len=ctx_len, with_ctx=with_ctx)
        x_all = _post(oa, ob, oc, x_all, mod, w_o_b, w_ffn_b, w_out_b, ln1g, ln1b, ln2g, ln2b,
                      layer=l, n_lat=n_lat, lat_per_batch=lat_per_batch, n_batch=n_batch,
                      n_tiles=n_tiles if with_ctx else n_lat)
    return x_all.reshape(n_batch, seq, D_MODEL)
```

```python
import functools
import math

import jax
import jax.numpy as jnp
import numpy as np
from jax import lax
from jax.experimental import pallas as pl
from jax.experimental.pallas import tpu as pltpu

F32 = jnp.float32
BF16 = jnp.bfloat16

D_MODEL = 1024
HEAD_DIM = 64
GRID_W = 64
NA_HEADS = 6
NA_WIN_R = 8
NA_WIN_C = 16
SG_GROUPS = 4
SG_CHUNK = 128
GQA_Q_HEADS = 6
GQA_KV_HEADS = 2
ROPE_THETA = 10000.0
MODEL_DEPTH = 4

D_A = NA_HEADS * HEAD_DIM
D_B = SG_GROUPS * HEAD_DIM
D_C = GQA_Q_HEADS * HEAD_DIM
D_KV_C = GQA_KV_HEADS * HEAD_DIM
D_IN = 3 * D_A + 2 * D_B + D_C + 2 * D_KV_C
D_FF = int(math.ceil(8 * D_MODEL / 3 / 256)) * 256
N_MOD = 6
ALPHA = (2 * MODEL_DEPTH) ** 0.25
LN_EPS = 1e-6
ATTN_SCALE = HEAD_DIM ** -0.5
MASK_VALUE = -1e30
LOG2_E = 1.4426950408889634
SAFE_SCORE_BOUND = 57.0

OFF_QA, OFF_KA, OFF_VA = 0, D_A, 2 * D_A
OFF_ZB = 3 * D_A
OFF_QC = OFF_ZB + 2 * D_B
OFF_KC = OFF_QC + D_C
OFF_VC = OFF_KC + D_KV_C

LANES = 128
TM = 512
TQ = 256
TK = 512
NA_ROWS_PER_STEP = TQ // GRID_W
NA_UNION_ROWS = 12
FF_CHUNK = 256
MOD_TN = 1536
VMEM_LIMIT = 56 * 1024 * 1024

GQA_HEAD_ORDER = (0, 3, 1, 4, 2, 5)


def _dot(a, b):
    return jnp.dot(a, b, preferred_element_type=F32)


def _dot_nt(a, b):
    return lax.dot_general(a, b, (((1,), (1,)), ((), ())), preferred_element_type=F32)


def _layer_norm(x):
    mu = jnp.mean(x, -1, keepdims=True)
    xc = x - mu
    var = jnp.mean(xc * xc, -1, keepdims=True)
    return xc * lax.rsqrt(var + LN_EPS)


def _rms(x):
    return x * lax.rsqrt(jnp.mean(x * x, -1, keepdims=True) + LN_EPS)


def _gelu_tanh(x):
    return x * (0.5 * (1.0 + jnp.tanh(0.7978845608028654 * (x + 0.044715 * (x * x * x)))))


def _silu(x):
    return x * (1.0 / (1.0 + jnp.exp(-x)))


def _low_half(shape):
    return lax.broadcasted_iota(jnp.int32, shape, 1) < HEAD_DIM


def _pair_rms(xg, gain):
    lo = _low_half(xg.shape)
    sq = xg * xg
    s_lo = jnp.sum(jnp.where(lo, sq, 0.0), -1, keepdims=True)
    s_hi = jnp.sum(jnp.where(lo, 0.0, sq), -1, keepdims=True)
    r = jnp.where(lo, lax.rsqrt(s_lo * (1.0 / HEAD_DIM) + LN_EPS),
                  lax.rsqrt(s_hi * (1.0 / HEAD_DIM) + LN_EPS))
    return xg * r * gain


def _rope(xn, cos_t, sin_t):
    lane = lax.broadcasted_iota(jnp.int32, xn.shape, 1)
    first = (lane % 32) < 16
    partner = jnp.where(first, pltpu.roll(xn, LANES - 16, 1), pltpu.roll(xn, 16, 1))
    return xn * cos_t + partner * sin_t


def _mod_kernel(c_ref, w_ref, b_ref, o_ref):
    sc = _silu(c_ref[...]).astype(BF16)
    o_ref[...] = _dot(sc, w_ref[...].astype(BF16)) + b_ref[...]


def _modulation(c_all, w_mod, b_mod):
    depth = w_mod.shape[0]
    n_out = w_mod.shape[2]
    return pl.pallas_call(
        _mod_kernel,
        grid=(depth, n_out // MOD_TN),
        in_specs=[
            pl.BlockSpec((8, D_MODEL), lambda l, j: (0, 0)),
            pl.BlockSpec((None, D_MODEL, MOD_TN), lambda l, j: (l, 0, j)),
            pl.BlockSpec((None, 1, MOD_TN), lambda l, j: (l, 0, j)),
        ],
        out_specs=pl.BlockSpec((None, 8, MOD_TN), lambda l, j: (l, 0, j)),
        out_shape=jax.ShapeDtypeStruct((depth, 8, n_out), F32),
        compiler_params=pltpu.CompilerParams(
            dimension_semantics=("arbitrary", "arbitrary"), vmem_limit_bytes=VMEM_LIMIT),
        name="modulation",
    )(c_all, w_mod, b_mod.reshape(depth, 1, n_out))


def _inproj_kernel(x_ref, mod_ref, w_ref, cos_ref, sin_ref, wsc_ref, bsg_ref, gsgu_ref,
                   gq_ref, gk_ref, gob_ref,
                   qa_ref, ka_ref, va_ref, ob_ref, qc_ref, kc_ref, vc_ref, y_even_ref, y_odd_ref):
    t = pl.program_id(0)

    @pl.when(t == 0)
    def _no_previous_tile():
        y_odd_ref[...] = jnp.zeros_like(y_odd_ref)

    @pl.when(t % 2 == 0)
    def _even():
        _inproj_step(x_ref, mod_ref, w_ref, cos_ref, sin_ref, wsc_ref, bsg_ref, gsgu_ref, gq_ref, gk_ref, gob_ref,
                     qa_ref, ka_ref, va_ref, ob_ref, qc_ref, kc_ref, vc_ref, y_even_ref, y_odd_ref)

    @pl.when(t % 2 == 1)
    def _odd():
        _inproj_step(x_ref, mod_ref, w_ref, cos_ref, sin_ref, wsc_ref, bsg_ref, gsgu_ref, gq_ref, gk_ref, gob_ref,
                     qa_ref, ka_ref, va_ref, ob_ref, qc_ref, kc_ref, vc_ref, y_odd_ref, y_even_ref)


def _inproj_step(x_ref, mod_ref, w_ref, cos_ref, sin_ref, wsc_ref, bsg_ref, gsgu_ref, gq_ref, gk_ref, gob_ref,
                 qa_ref, ka_ref, va_ref, ob_ref, qc_ref, kc_ref, vc_ref, y_new_ref, y_ref):
    shift = mod_ref[0, :, 0:D_MODEL]
    scale = mod_ref[0, :, D_MODEL:2 * D_MODEL]
    h = (_layer_norm(x_ref[...]) * (1.0 + scale) + shift).astype(BF16)
    y_new_ref[...] = _dot(h, w_ref[...])

    def proj(off, width):
        return y_ref[:, off:off + width]

    qa_ref[...] = (proj(OFF_QA, D_A) * ATTN_SCALE).astype(BF16)
    ka_ref[...] = proj(OFF_KA, D_A).astype(BF16)
    va_ref[...] = proj(OFF_VA, D_A).astype(BF16)

    u = _gelu_tanh(proj(OFF_ZB, D_B))
    v = (_layer_norm(_gelu_tanh(proj(OFF_ZB + D_B, D_B))) * gsgu_ref[...]).astype(BF16)
    lane_group = lax.broadcasted_iota(jnp.int32, (SG_CHUNK, D_B), 1) // HEAD_DIM
    for c in range(TM // SG_CHUNK):
        rows = slice(c * SG_CHUNK, (c + 1) * SG_CHUNK)
        vch = v[rows]
        rhs = jnp.concatenate(
            [jnp.where(lane_group == g, vch, jnp.zeros_like(vch)) for g in range(SG_GROUPS)], 0)
        mixed = _dot(wsc_ref[...], rhs) + bsg_ref[...]
        ob_ref[rows, :] = (_rms(u[rows] * mixed) * gob_ref[...]).astype(BF16)

    cos_t = cos_ref[...]
    sin_t = sin_ref[...]
    for p in range(D_C // LANES):
        xq = proj(OFF_QC + p * LANES, LANES)
        xq = _rope(_pair_rms(xq, gq_ref[...]), cos_t, sin_t)
        qc_ref[:, p * LANES:(p + 1) * LANES] = (xq * (ATTN_SCALE * LOG2_E)).astype(BF16)
    xk = _rope(_pair_rms(proj(OFF_KC, D_KV_C), gk_ref[...]), cos_t, sin_t)
    kc_ref[...] = xk.astype(BF16)
    vc_ref[...] = proj(OFF_VC, D_KV_C).astype(BF16)


def _layer_spec(layer, shape):
    zeros = (0,) * len(shape)
    return pl.BlockSpec((None,) + tuple(shape), lambda *_: (layer,) + zeros)


def _inproj(x_all, mod, w_in_b, cos_t, sin_t, wsc, bsg, gsgu, gq2, gk2, gob, *, layer, n_lat, lat_per_batch,
            n_batch, rope_lat_tiles):
    t_rows = x_all.shape[0]
    n_tiles = t_rows // TM

    def in_map(t):
        return (jnp.minimum(t, n_tiles - 1), 0)

    def row_map(t):
        return (jnp.maximum(t - 1, 0), 0)

    def mod_map(t):
        tile = jnp.minimum(t, n_tiles - 1)
        return (layer, jnp.where(tile < n_lat, tile // lat_per_batch, n_batch), 0, 0)

    def rope_map(t):
        tile = jnp.maximum(t - 1, 0)
        return (jnp.where(tile < n_lat, tile % rope_lat_tiles, rope_lat_tiles), 0)

    widths = (D_A, D_A, D_A, D_B, D_C, D_KV_C, D_KV_C)
    return pl.pallas_call(
        _inproj_kernel,
        grid=(n_tiles + 1,),
        in_specs=[
            pl.BlockSpec((TM, D_MODEL), in_map),
            pl.BlockSpec((None, 1, 1, N_MOD * D_MODEL), mod_map),
            _layer_spec(layer, (D_MODEL, D_IN)),
            pl.BlockSpec((TM, LANES), rope_map),
            pl.BlockSpec((TM, LANES), rope_map),
            _layer_spec(layer, (SG_CHUNK, SG_GROUPS * SG_CHUNK)),
            _layer_spec(layer, (SG_CHUNK, D_B)),
            _layer_spec(layer, (1, D_B)),
            _layer_spec(layer, (1, LANES)),
            _layer_spec(layer, (1, LANES)),
            _layer_spec(layer, (1, D_B)),
        ],
        out_specs=[pl.BlockSpec((TM, w), row_map) for w in widths],
        out_shape=[jax.ShapeDtypeStruct((t_rows, w), BF16) for w in widths],
        scratch_shapes=[pltpu.VMEM((TM, D_IN), F32), pltpu.VMEM((TM, D_IN), F32)],
        compiler_params=pltpu.CompilerParams(
            dimension_semantics=("arbitrary",), vmem_limit_bytes=VMEM_LIMIT),
        name="inproj",
    )(x_all, mod, w_in_b, cos_t, sin_t, wsc, bsg, gsgu, gq2, gk2, gob)


def _split_heads(qg):
    lo = _low_half(qg.shape)
    zero = jnp.zeros_like(qg)
    return jnp.concatenate([jnp.where(lo, qg, zero), jnp.where(lo, zero, qg)], 0)


def _join_heads(o, m):
    return jnp.where(_low_half((m, LANES)), o[0:m], o[m:2 * m])


def _write_merged(o_ref, rows, outs, gout_ref):
    width = len(outs) * LANES
    ss = outs[0] * outs[0]
    for o in outs[1:]:
        ss = ss + o * o
    r = lax.rsqrt(jnp.sum(ss, -1, keepdims=True) * (1.0 / width) + LN_EPS)
    for p, o in enumerate(outs):
        cols = slice(p * LANES, (p + 1) * LANES)
        o_ref[rows, cols] = (o * r * gout_ref[:, cols]).astype(BF16)


def _na_latent(q_ref, k_ref, v_ref, kx_ref, vx_ref, bias_ref, gout_ref, o_ref, t, grid_rows):
    span = NA_UNION_ROWS * GRID_W
    first = jnp.clip(t * NA_ROWS_PER_STEP - NA_WIN_R // 2, 0, grid_rows - NA_UNION_ROWS)
    start = pl.multiple_of(first * GRID_W, GRID_W)
    outs = []
    for p in range(D_A // LANES):
        cols = slice(p * LANES, (p + 1) * LANES)
        lhs = _split_heads(q_ref[:, cols])
        s_loc = _dot_nt(lhs, k_ref[pl.ds(start, span), cols]) + bias_ref[0, p]
        s_ctx = _dot_nt(lhs, kx_ref[:, cols])
        m = jnp.maximum(jnp.max(s_loc, -1, keepdims=True), jnp.max(s_ctx, -1, keepdims=True))
        e_loc = jnp.exp(s_loc - m)
        e_ctx = jnp.exp(s_ctx - m)
        l = jnp.sum(e_loc, -1, keepdims=True) + jnp.sum(e_ctx, -1, keepdims=True)
        o = _dot(e_loc.astype(BF16), v_ref[pl.ds(start, span), cols]) \
            + _dot(e_ctx.astype(BF16), vx_ref[:, cols])
        outs.append(_join_heads(o * (1.0 / l), TQ))
    _write_merged(o_ref, slice(0, TQ), outs, gout_ref)


def _na_context(q_ref, kx_ref, vx_ref, gout_ref, o_ref):
    outs = []
    for p in range(D_A // LANES):
        cols = slice(p * LANES, (p + 1) * LANES)
        lhs = _split_heads(q_ref[:, cols])
        s = _dot_nt(lhs, kx_ref[:, cols])
        e = jnp.exp(s - jnp.max(s, -1, keepdims=True))
        o = _dot(e.astype(BF16), vx_ref[:, cols]) * (1.0 / jnp.sum(e, -1, keepdims=True))
        outs.append(_join_heads(o, TQ))
    _write_merged(o_ref, slice(0, TQ), outs, gout_ref)


def _gqa_fixed_shift(lhs_ref, k_ref, v_ref, kx_ref, vx_ref, l_ref, acc_ref, score_bound, n_k_chunks):
    def chunk(k_c, v_c):
        p = jnp.exp2(_dot_nt(lhs_ref[...], k_c) - score_bound)
        l_part = p[:, 0:LANES]
        for i in range(1, k_c.shape[0] // LANES):
            l_part = l_part + p[:, i * LANES:(i + 1) * LANES]
        return l_part, _dot(p.astype(BF16), v_c)

    l0, a0 = chunk(kx_ref[...], vx_ref[...])
    l_ref[...] = l0
    acc_ref[...] = a0
    for c in range(n_k_chunks):
        l_c, a_c = chunk(k_ref[c * TK:(c + 1) * TK, :], v_ref[c * TK:(c + 1) * TK, :])
        l_ref[...] += l_c
        acc_ref[...] += a_c
    acc_ref[...] = acc_ref[...] * (1.0 / jnp.sum(l_ref[...], -1, keepdims=True))


def _gqa_running_max(lhs_ref, k_ref, v_ref, kx_ref, vx_ref, m_ref, l_ref, acc_ref, latent, n_k_chunks):
    m_rows = GQA_Q_HEADS * TQ
    s = _dot_nt(lhs_ref[...], kx_ref[...])
    m0 = jnp.max(s, -1, keepdims=True)
    e = jnp.exp2(s - m0)
    m_ref[...] = jnp.broadcast_to(m0, (m_rows, LANES))
    l_ref[...] = jnp.broadcast_to(jnp.sum(e, -1, keepdims=True), (m_rows, LANES))
    acc_ref[...] = _dot(e.astype(BF16), vx_ref[...])

    @pl.when(latent)
    def _latent_keys():
        def body(c, carry):
            start = pl.multiple_of(c * TK, TK)
            s = _dot_nt(lhs_ref[...], k_ref[pl.ds(start, TK), :])
            m_prev = m_ref[:, 0:1]
            m_new = jnp.maximum(m_prev, jnp.max(s, -1, keepdims=True))
            alpha = jnp.exp2(m_prev - m_new)
            e = jnp.exp2(s - m_new)
            l_new = alpha * l_ref[:, 0:1] + jnp.sum(e, -1, keepdims=True)
            acc_ref[...] = alpha * acc_ref[...] + _dot(e.astype(BF16), v_ref[pl.ds(start, TK), :])
            m_ref[...] = jnp.broadcast_to(m_new, (m_rows, LANES))
            l_ref[...] = jnp.broadcast_to(l_new, (m_rows, LANES))
            return carry

        lax.fori_loop(0, n_k_chunks, body, 0)

    acc_ref[...] = acc_ref[...] * (1.0 / l_ref[:, 0:1])


def _attn_kernel(qa_ref, ka_ref, va_ref, kax_ref, vax_ref, bias_ref, gouta_ref,
                 qc_ref, kc_ref, vc_ref, kcx_ref, vcx_ref, gq_ref, gk_ref, goutc_ref,
                 oa_ref, oc_ref, lhs_ref, m_ref, l_ref, acc_ref, *, grid_rows, n_blocks, n_k_chunks):
    t = pl.program_id(1)
    latent = t < n_blocks
    n_groups = D_C // LANES

    for j in range(GQA_KV_HEADS):
        for g in range(n_groups):
            qg = qc_ref[:, g * LANES:(g + 1) * LANES]
            lo = _low_half(qg.shape)
            keep = lo if j == 0 else jnp.logical_not(lo)
            lhs_ref[(j * n_groups + g) * TQ:(j * n_groups + g + 1) * TQ, :] = jnp.where(keep, qg, jnp.zeros_like(qg))

    score_bound = (HEAD_DIM * ATTN_SCALE * LOG2_E) * jnp.max(jnp.abs(gq_ref[...])) * jnp.max(jnp.abs(gk_ref[...]))
    bounded = score_bound <= SAFE_SCORE_BOUND
    main_path = jnp.logical_and(bounded, latent)

    @pl.when(main_path)
    def _latent_bounded():
        _gqa_fixed_shift(lhs_ref, kc_ref, vc_ref, kcx_ref, vcx_ref, l_ref, acc_ref, score_bound, n_k_chunks)
        _na_latent(qa_ref, ka_ref, va_ref, kax_ref, vax_ref, bias_ref, gouta_ref, oa_ref, t, grid_rows)

    @pl.when(jnp.logical_not(main_path))
    def _other():
        @pl.when(latent)
        def _():
            _na_latent(qa_ref, ka_ref, va_ref, kax_ref, vax_ref, bias_ref, gouta_ref, oa_ref, t, grid_rows)

        @pl.when(jnp.logical_not(latent))
        def _():
            _na_context(qa_ref, kax_ref, vax_ref, gouta_ref, oa_ref)

        @pl.when(bounded)
        def _():
            _gqa_fixed_shift(lhs_ref, kc_ref, vc_ref, kcx_ref, vcx_ref, l_ref, acc_ref, score_bound, 0)

        @pl.when(jnp.logical_not(bounded))
        def _():
            _gqa_running_max(lhs_ref, kc_ref, vc_ref, kcx_ref, vcx_ref, m_ref, l_ref, acc_ref, latent, n_k_chunks)

    lo = _low_half((TQ, LANES))
    outs = [jnp.where(lo, acc_ref[g * TQ:(g + 1) * TQ, :], acc_ref[(n_groups + g) * TQ:(n_groups + g + 1) * TQ, :])
            for g in range(n_groups)]
    _write_merged(oc_ref, slice(0, TQ), outs, goutc_ref)


def _attn(qa, ka, va, bias, gout_a, qc, kc, vc, gq2, gk2, gout_c, *, layer, n_batch, seq, ctx_len, with_ctx):
    grid_rows = seq // GRID_W
    n_blocks = seq // TQ
    assert grid_rows >= NA_UNION_ROWS + NA_ROWS_PER_STEP
    ctx_block0 = n_batch * seq // ctx_len
    q_ctx_block0 = n_batch * seq // TQ
    out_rows = qa.shape[0] if with_ctx else n_batch * seq
    m_rows = GQA_Q_HEADS * TQ

    def q_map(b, t):
        return (jnp.where(t < n_blocks, b * n_blocks + t, q_ctx_block0 + b), 0)

    def batch_map(b, t):
        return (b, 0)

    def ctx_map(b, t):
        return (ctx_block0 + b, 0)

    def bias_map(b, t):
        return (layer, (t > 0).astype(jnp.int32) + (t >= n_blocks - 1).astype(jnp.int32), 0, 0, 0)

    single = pl.Buffered(1)
    return pl.pallas_call(
        functools.partial(_attn_kernel, grid_rows=grid_rows, n_blocks=n_blocks, n_k_chunks=seq // TK),
        grid=(n_batch, n_blocks + (1 if with_ctx else 0)),
        in_specs=[
            pl.BlockSpec((TQ, D_A), q_map),
            pl.BlockSpec((seq, D_A), batch_map, pipeline_mode=single),
            pl.BlockSpec((seq, D_A), batch_map, pipeline_mode=single),
            pl.BlockSpec((ctx_len, D_A), ctx_map),
            pl.BlockSpec((ctx_len, D_A), ctx_map),
            pl.BlockSpec((None, 1) + bias.shape[2:], bias_map, pipeline_mode=single),
            _layer_spec(layer, (1, D_A)),
            pl.BlockSpec((TQ, D_C), q_map),
            pl.BlockSpec((seq, D_KV_C), batch_map),
            pl.BlockSpec((seq, D_KV_C), batch_map),
            pl.BlockSpec((ctx_len, D_KV_C), ctx_map),
            pl.BlockSpec((ctx_len, D_KV_C), ctx_map),
            _layer_spec(layer, (1, LANES)),
            _layer_spec(layer, (1, LANES)),
            _layer_spec(layer, (1, D_C)),
        ],
        out_specs=[pl.BlockSpec((TQ, D_A), q_map), pl.BlockSpec((TQ, D_C), q_map)],
        out_shape=[jax.ShapeDtypeStruct((out_rows, D_A), BF16), jax.ShapeDtypeStruct((out_rows, D_C), BF16)],
        scratch_shapes=[
            pltpu.VMEM((m_rows, LANES), BF16),
            pltpu.VMEM((m_rows, LANES), F32),
            pltpu.VMEM((m_rows, LANES), F32),
            pltpu.VMEM((m_rows, LANES), F32),
        ],
        compiler_params=pltpu.CompilerParams(
            dimension_semantics=("arbitrary", "arbitrary"), vmem_limit_bytes=VMEM_LIMIT),
        name="attn",
    )(qa, ka, va, ka, va, bias, gout_a, qc, kc, vc, kc, vc, gq2, gk2, gout_c)


def _post_kernel(oa_ref, ob_ref, oc_ref, x_ref, mod_ref, wo_ref, wffn_ref, wout_ref,
                 ln1g_ref, ln1b_ref, ln2g_ref, ln2b_ref, o_ref, h_ref, g_ref):
    def mod(i):
        return mod_ref[0, :, i * D_MODEL:(i + 1) * D_MODEL]

    y = (_dot(oa_ref[...], wo_ref[0:D_A, :]) + _dot(ob_ref[...], wo_ref[D_A:D_A + D_B, :])
         + _dot(oc_ref[...], wo_ref[D_A + D_B:, :]))
    x1 = _layer_norm(ALPHA * x_ref[...] + mod(2) * y) * ln1g_ref[...] + ln1b_ref[...]
    h_ref[...] = (_layer_norm(x1) * (1.0 + mod(4)) + mod(3)).astype(BF16)
    for c in range(D_FF // FF_CHUNK):
        cols = slice(c * FF_CHUNK, (c + 1) * FF_CHUNK)
        gate = _dot(h_ref[...], wffn_ref[:, cols])
        up = _dot(h_ref[...], wffn_ref[:, D_FF + c * FF_CHUNK:D_FF + (c + 1) * FF_CHUNK])
        g_ref[:, cols] = (_silu(gate) * up).astype(BF16)
    ff = _dot(g_ref[...], wout_ref[...])
    o_ref[...] = _layer_norm(ALPHA * x1 + mod(5) * ff) * ln2g_ref[...] + ln2b_ref[...]


def _post(oa, ob, oc, x_all, mod, wo, wffn, wout, ln1g, ln1b, ln2g, ln2b, *, layer, n_lat, lat_per_batch,
          n_batch, n_tiles):
    def row_map(t):
        return (t, 0)

    def mod_map(t):
        return (layer, jnp.where(t < n_lat, t // lat_per_batch, n_batch), 0, 0)

    def resident(shape):
        return pl.BlockSpec((None,) + shape, lambda t: (layer, 0, 0), pipeline_mode=pl.Buffered(1))

    return pl.pallas_call(
        _post_kernel,
        grid=(n_tiles,),
        in_specs=[
            pl.BlockSpec((TM, D_A), row_map),
            pl.BlockSpec((TM, D_B), row_map),
            pl.BlockSpec((TM, D_C), row_map),
            pl.BlockSpec((TM, D_MODEL), row_map),
            pl.BlockSpec((None, 1, 1, N_MOD * D_MODEL), mod_map),
            resident((D_MODEL, D_MODEL)),
            resident((D_MODEL, 2 * D_FF)),
            resident((D_FF, D_MODEL)),
            _layer_spec(layer, (1, D_MODEL)),
            _layer_spec(layer, (1, D_MODEL)),
            _layer_spec(layer, (1, D_MODEL)),
            _layer_spec(layer, (1, D_MODEL)),
        ],
        out_specs=pl.BlockSpec((TM, D_MODEL), row_map),
        out_shape=jax.ShapeDtypeStruct((n_tiles * TM, D_MODEL), F32),
        scratch_shapes=[pltpu.VMEM((TM, D_MODEL), BF16), pltpu.VMEM((TM, D_FF), BF16)],
        compiler_params=pltpu.CompilerParams(
            dimension_semantics=("arbitrary",), vmem_limit_bytes=VMEM_LIMIT),
        name="post",
    )(oa, ob, oc, x_all, mod, wo, wffn, wout, ln1g, ln1b, ln2g, ln2b)


def _rope_tables(seq):
    t = jnp.arange(seq, dtype=jnp.int32)
    row = (t // GRID_W).astype(F32)
    col = (t % GRID_W).astype(F32)
    n_freq = HEAD_DIM // 4
    inv = 1.0 / (ROPE_THETA ** (jnp.arange(n_freq, dtype=F32) / n_freq))
    ang = jnp.stack([row[:, None] * inv, col[:, None] * inv], axis=1)
    cos = jnp.cos(ang)[:, :, None, :]
    sin = jnp.sin(ang)[:, :, None, :]
    cos_h = jnp.broadcast_to(cos, (seq, 2, 2, n_freq)).reshape(seq, HEAD_DIM)
    sign = jnp.array([-1.0, 1.0], F32)[None, None, :, None]
    sin_h = jnp.broadcast_to(sin * sign, (seq, 2, 2, n_freq)).reshape(seq, HEAD_DIM)
    cos_t = jnp.concatenate([jnp.tile(cos_h, (1, 2)), jnp.ones((TM, LANES), F32)], 0)
    sin_t = jnp.concatenate([jnp.tile(sin_h, (1, 2)), jnp.zeros((TM, LANES), F32)], 0)
    return cos_t, sin_t


def _na_bias_tables(rpb):
    depth = rpb.shape[0]
    c_idx = np.arange(GRID_W)
    col_start = np.clip(c_idx - NA_WIN_C // 2, 0, GRID_W - NA_WIN_C)
    kc = np.arange(GRID_W)
    col_ok = (kc[None, :] >= col_start[:, None]) & (kc[None, :] < col_start[:, None] + NA_WIN_C)
    pad = GRID_W - NA_WIN_C
    padded = jnp.pad(rpb, ((0, 0), (0, 0), (0, 0), (pad, pad)))
    toep = jnp.stack([padded[..., GRID_W - 1 - c:2 * GRID_W - 1 - c] for c in range(GRID_W)], axis=-2)
    toep = jnp.where(jnp.asarray(col_ok), toep, MASK_VALUE)
    lead = np.array([0, NA_WIN_R // 2, NA_UNION_ROWS - NA_ROWS_PER_STEP])[:, None, None]
    i = np.arange(NA_ROWS_PER_STEP)[None, :, None]
    j = np.arange(NA_UNION_ROWS)[None, None, :]
    win_first = np.stack([np.zeros((NA_ROWS_PER_STEP, 1), np.int64), np.arange(NA_ROWS_PER_STEP)[:, None],
                          np.full((NA_ROWS_PER_STEP, 1), NA_UNION_ROWS - NA_WIN_R)])
    row_ok = (j >= win_first) & (j < win_first + NA_WIN_R)
    dr = j - lead - i + (NA_WIN_R - 1)
    masked = jnp.full((depth, NA_HEADS, GRID_W, GRID_W), MASK_VALUE, F32)
    variants = []
    for v in range(3):
        per_row = []
        for qi in range(NA_ROWS_PER_STEP):
            blocks = [toep[:, :, int(dr[v, qi, kj])] if row_ok[v, qi, kj] else masked
                      for kj in range(NA_UNION_ROWS)]
            per_row.append(jnp.concatenate(blocks, axis=-1))
        variants.append(jnp.concatenate(per_row, axis=-2))
    tab = jnp.stack(variants, axis=1)
    return tab.reshape(depth, 3, NA_HEADS // 2, 2 * TQ, NA_UNION_ROWS * GRID_W)


def kernel(x, c, ctx, c_ctx, w_mod, b_mod, w_in, rpb, w_s, b_s, g_sgu, g_q, g_k, g_out, w_o,
           ln1_g, ln1_b, w_ffn_in, w_ffn_out, ln2_g, ln2_b):
    n_batch, seq, _ = x.shape
    ctx_len = ctx.shape[1]
    depth = w_in.shape[0]
    assert seq % TM == 0 and (n_batch * ctx_len) % TM == 0 and ctx_len == TQ and seq % TK == 0
    assert n_batch + 1 <= 8
    n_lat = n_batch * seq // TM
    n_tiles = n_lat + n_batch * ctx_len // TM
    lat_per_batch = seq // TM

    x_all = jnp.concatenate([x.reshape(n_batch * seq, D_MODEL), ctx.reshape(n_batch * ctx_len, D_MODEL)], 0)

    c_all = jnp.zeros((8, D_MODEL), F32).at[:n_batch].set(c).at[n_batch].set(c_ctx)
    mod = _modulation(c_all, w_mod, b_mod)[:, :n_batch + 1].reshape(depth, n_batch + 1, 1, N_MOD * D_MODEL)

    def reorder_gqa_heads(a, axis, off):
        parts = [lax.slice_in_dim(a, 0, off, axis=axis)]
        parts += [lax.slice_in_dim(a, off + h * HEAD_DIM, off + (h + 1) * HEAD_DIM, axis=axis)
                  for h in GQA_HEAD_ORDER]
        parts.append(lax.slice_in_dim(a, off + D_C, a.shape[axis], axis=axis))
        return jnp.concatenate(parts, axis)

    def rows(v):
        return v.reshape(depth, 1, -1)

    w_in_b = reorder_gqa_heads(w_in, 2, OFF_QC).astype(BF16)
    w_o_b = reorder_gqa_heads(w_o, 1, D_A + D_B).astype(BF16)
    g_out_p = reorder_gqa_heads(g_out, 1, D_A + D_B)
    g_out_a, g_out_b, g_out_c = (rows(g_out_p[:, :D_A]), rows(g_out_p[:, D_A:D_A + D_B]),
                                 rows(g_out_p[:, D_A + D_B:]))
    w_ffn_b = w_ffn_in.astype(BF16)
    w_out_b = w_ffn_out.astype(BF16)
    w_s_cat = jnp.transpose(w_s, (0, 2, 1, 3)).reshape(depth, SG_CHUNK, SG_GROUPS * SG_CHUNK).astype(BF16)
    b_s_full = jnp.repeat(jnp.transpose(b_s, (0, 2, 1)), HEAD_DIM, axis=2)
    g_sgu3 = rows(g_sgu)
    g_q2 = rows(jnp.tile(g_q, (1, 2)))
    g_k2 = rows(jnp.tile(g_k, (1, 2)))
    ln1g, ln1b, ln2g, ln2b = rows(ln1_g), rows(ln1_b), rows(ln2_g), rows(ln2_b)
    cos_t, sin_t = _rope_tables(seq)
    bias = _na_bias_tables(rpb)

    for l in range(depth):
        with_ctx = l < depth - 1
        qa, ka, va, ob, qc, kc, vc = _inproj(
            x_all, mod, w_in_b, cos_t, sin_t, w_s_cat, b_s_full, g_sgu3, g_q2, g_k2, g_out_b,
            layer=l, n_lat=n_lat, lat_per_batch=lat_per_batch, n_batch=n_batch, rope_lat_tiles=seq // TM)
        oa, oc = _attn(qa, ka, va, bias, g_out_a, qc, kc, vc, g_q2, g_k2, g_out_c,
                       layer=l, n_batch=n_batch, seq=seq, ctx_len=ctx_len, with_ctx=with_ctx)
        x_all = _post(oa, ob, oc, x_all, mod, w_o_b, w_ffn_b, w_out_b, ln1g, ln1b, ln2g, ln2b,
                      layer=l, n_lat=n_lat, lat_per_batch=lat_per_batch, n_batch=n_batch,
                      n_tiles=n_tiles if with_ctx else n_lat)
    return x_all.reshape(n_batch, seq, D_MODEL)
```

```python
import functools
import math

import jax
import jax.numpy as jnp
import numpy as np
from jax import lax
from jax.experimental import pallas as pl
from jax.experimental.pallas import tpu as pltpu

F32 = jnp.float32
BF16 = jnp.bfloat16

D_MODEL = 1024
HEAD_DIM = 64
GRID_W = 64
NA_HEADS = 6
NA_WIN_R = 8
NA_WIN_C = 16
SG_GROUPS = 4
SG_CHUNK = 128
GQA_Q_HEADS = 6
GQA_KV_HEADS = 2
ROPE_THETA = 10000.0
MODEL_DEPTH = 4

D_A = NA_HEADS * HEAD_DIM
D_B = SG_GROUPS * HEAD_DIM
D_C = GQA_Q_HEADS * HEAD_DIM
D_KV_C = GQA_KV_HEADS * HEAD_DIM
D_IN = 3 * D_A + 2 * D_B + D_C + 2 * D_KV_C
D_FF = int(math.ceil(8 * D_MODEL / 3 / 256)) * 256
N_MOD = 6
ALPHA = (2 * MODEL_DEPTH) ** 0.25
LN_EPS = 1e-6
ATTN_SCALE = HEAD_DIM ** -0.5
MASK_VALUE = -1e30
LOG2_E = 1.4426950408889634
SAFE_SCORE_BOUND = 57.0

OFF_QA, OFF_KA, OFF_VA = 0, D_A, 2 * D_A
OFF_ZB = 3 * D_A
OFF_QC = OFF_ZB + 2 * D_B
OFF_KC = OFF_QC + D_C
OFF_VC = OFF_KC + D_KV_C

LANES = 128
TM = 512
TQ = 256
TK = 512
NA_ROWS_PER_STEP = TQ // GRID_W
NA_UNION_ROWS = 12
FF_CHUNK = 256
MOD_TN = 1536
VMEM_LIMIT = 56 * 1024 * 1024

GQA_HEAD_ORDER = (0, 3, 1, 4, 2, 5)


def _dot(a, b):
    return jnp.dot(a, b, preferred_element_type=F32)


def _dot_nt(a, b):
    return lax.dot_general(a, b, (((1,), (1,)), ((), ())), preferred_element_type=F32)


def _layer_norm(x):
    mu = jnp.mean(x, -1, keepdims=True)
    xc = x - mu
    var = jnp.mean(xc * xc, -1, keepdims=True)
    return xc * lax.rsqrt(var + LN_EPS)


def _rms(x):
    return x * lax.rsqrt(jnp.mean(x * x, -1, keepdims=True) + LN_EPS)


def _gelu_tanh(x):
    return x * (0.5 * (1.0 + jnp.tanh(0.7978845608028654 * (x + 0.044715 * (x * x * x)))))


def _silu(x):
    return x * (1.0 / (1.0 + jnp.exp(-x)))


def _low_half(shape):
    return lax.broadcasted_iota(jnp.int32, shape, 1) < HEAD_DIM


def _pair_rms(xg, gain):
    lo = _low_half(xg.shape)
    sq = xg * xg
    s_lo = jnp.sum(jnp.where(lo, sq, 0.0), -1, keepdims=True)
    s_hi = jnp.sum(jnp.where(lo, 0.0, sq), -1, keepdims=True)
    r = jnp.where(lo, lax.rsqrt(s_lo * (1.0 / HEAD_DIM) + LN_EPS),
                  lax.rsqrt(s_hi * (1.0 / HEAD_DIM) + LN_EPS))
    return xg * r * gain


def _rope(xn, cos_t, sin_t):
    lane = lax.broadcasted_iota(jnp.int32, xn.shape, 1)
    first = (lane % 32) < 16
    partner = jnp.where(first, pltpu.roll(xn, LANES - 16, 1), pltpu.roll(xn, 16, 1))
    return xn * cos_t + partner * sin_t


def _mod_kernel(c_ref, w_ref, b_ref, o_ref):
    sc = _silu(c_ref[...]).astype(BF16)
    o_ref[...] = _dot(sc, w_ref[...].astype(BF16)) + b_ref[...]


def _modulation(c_all, w_mod, b_mod):
    depth = w_mod.shape[0]
    n_out = w_mod.shape[2]
    return pl.pallas_call(
        _mod_kernel,
        grid=(depth, n_out // MOD_TN),
        in_specs=[
            pl.BlockSpec((8, D_MODEL), lambda l, j: (0, 0)),
            pl.BlockSpec((None, D_MODEL, MOD_TN), lambda l, j: (l, 0, j)),
            pl.BlockSpec((None, 1, MOD_TN), lambda l, j: (l, 0, j)),
        ],
        out_specs=pl.BlockSpec((None, 8, MOD_TN), lambda l, j: (l, 0, j)),
        out_shape=jax.ShapeDtypeStruct((depth, 8, n_out), F32),
        compiler_params=pltpu.CompilerParams(
            dimension_semantics=("arbitrary", "arbitrary"), vmem_limit_bytes=VMEM_LIMIT),
        name="modulation",
    )(c_all, w_mod, b_mod.reshape(depth, 1, n_out))


def _inproj_kernel(x_ref, mod_ref, w_ref, cos_ref, sin_ref, wsc_ref, bsg_ref, gsgu_ref,
                   gq_ref, gk_ref, gob_ref,
                   qa_ref, ka_ref, va_ref, ob_ref, qc_ref, kc_ref, vc_ref, y_even_ref, y_odd_ref):
    t = pl.program_id(0)

    @pl.when(t == 0)
    def _no_previous_tile():
        y_odd_ref[...] = jnp.zeros_like(y_odd_ref)

    @pl.when(t % 2 == 0)
    def _even():
        _inproj_step(x_ref, mod_ref, w_ref, cos_ref, sin_ref, wsc_ref, bsg_ref, gsgu_ref, gq_ref, gk_ref, gob_ref,
                     qa_ref, ka_ref, va_ref, ob_ref, qc_ref, kc_ref, vc_ref, y_even_ref, y_odd_ref)

    @pl.when(t % 2 == 1)
    def _odd():
        _inproj_step(x_ref, mod_ref, w_ref, cos_ref, sin_ref, wsc_ref, bsg_ref, gsgu_ref, gq_ref, gk_ref, gob_ref,
                     qa_ref, ka_ref, va_ref, ob_ref, qc_ref, kc_ref, vc_ref, y_odd_ref, y_even_ref)


def _inproj_step(x_ref, mod_ref, w_ref, cos_ref, sin_ref, wsc_ref, bsg_ref, gsgu_ref, gq_ref, gk_ref, gob_ref,
                 qa_ref, ka_ref, va_ref, ob_ref, qc_ref, kc_ref, vc_ref, y_new_ref, y_ref):
    shift = mod_ref[0, :, 0:D_MODEL]
    scale = mod_ref[0, :, D_MODEL:2 * D_MODEL]
    h = (_layer_norm(x_ref[...]) * (1.0 + scale) + shift).astype(BF16)
    y_new_ref[...] = _dot(h, w_ref[...])

    def proj(off, width):
        return y_ref[:, off:off + width]

    qa_ref[...] = (proj(OFF_QA, D_A) * ATTN_SCALE).astype(BF16)
    ka_ref[...] = proj(OFF_KA, D_A).astype(BF16)
    va_ref[...] = proj(OFF_VA, D_A).astype(BF16)

    u = _gelu_tanh(proj(OFF_ZB, D_B))
    v = (_layer_norm(_gelu_tanh(proj(OFF_ZB + D_B, D_B))) * gsgu_ref[...]).astype(BF16)
    lane_group = lax.broadcasted_iota(jnp.int32, (SG_CHUNK, D_B), 1) // HEAD_DIM
    for c in range(TM // SG_CHUNK):
        rows = slice(c * SG_CHUNK, (c + 1) * SG_CHUNK)
        vch = v[rows]
        rhs = jnp.concatenate(
            [jnp.where(lane_group == g, vch, jnp.zeros_like(vch)) for g in range(SG_GROUPS)], 0)
        mixed = _dot(wsc_ref[...], rhs) + bsg_ref[...]
        ob_ref[rows, :] = (_rms(u[rows] * mixed) * gob_ref[...]).astype(BF16)

    cos_t = cos_ref[...]
    sin_t = sin_ref[...]
    for p in range(D_C // LANES):
        xq = proj(OFF_QC + p * LANES, LANES)
        xq = _rope(_pair_rms(xq, gq_ref[...]), cos_t, sin_t)
        qc_ref[:, p * LANES:(p + 1) * LANES] = (xq * (ATTN_SCALE * LOG2_E)).astype(BF16)
    xk = _rope(_pair_rms(proj(OFF_KC, D_KV_C), gk_ref[...]), cos_t, sin_t)
    kc_ref[...] = xk.astype(BF16)
    vc_ref[...] = proj(OFF_VC, D_KV_C).astype(BF16)


def _layer_spec(layer, shape):
    zeros = (0,) * len(shape)
    return pl.BlockSpec((None,) + tuple(shape), lambda *_: (layer,) + zeros)


def _inproj(x_all, mod, w_in_b, cos_t, sin_t, wsc, bsg, gsgu, gq2, gk2, gob, *, layer, n_lat, lat_per_batch,
            n_batch, rope_lat_tiles):
    t_rows = x_all.shape[0]
    n_tiles = t_rows // TM

    def in_map(t):
        return (jnp.minimum(t, n_tiles - 1), 0)

    def row_map(t):
        return (jnp.maximum(t - 1, 0), 0)

    def mod_map(t):
        tile = jnp.minimum(t, n_tiles - 1)
        return (layer, jnp.where(tile < n_lat, tile // lat_per_batch, n_batch), 0, 0)

    def rope_map(t):
        tile = jnp.maximum(t - 1, 0)
        return (jnp.where(tile < n_lat, tile % rope_lat_tiles, rope_lat_tiles), 0)

    widths = (D_A, D_A, D_A, D_B, D_C, D_KV_C, D_KV_C)
    return pl.pallas_call(
        _inproj_kernel,
        grid=(n_tiles + 1,),
        in_specs=[
            pl.BlockSpec((TM, D_MODEL), in_map),
            pl.BlockSpec((None, 1, 1, N_MOD * D_MODEL), mod_map),
            _layer_spec(layer, (D_MODEL, D_IN)),
            pl.BlockSpec((TM, LANES), rope_map),
            pl.BlockSpec((TM, LANES), rope_map),
            _layer_spec(layer, (SG_CHUNK, SG_GROUPS * SG_CHUNK)),
            _layer_spec(layer, (SG_CHUNK, D_B)),
            _layer_spec(layer, (1, D_B)),
            _layer_spec(layer, (1, LANES)),
            _layer_spec(layer, (1, LANES)),
            _layer_spec(layer, (1, D_B)),
        ],
        out_specs=[pl.BlockSpec((TM, w), row_map) for w in widths],
        out_shape=[jax.ShapeDtypeStruct((t_rows, w), BF16) for w in widths],
        scratch_shapes=[pltpu.VMEM((TM, D_IN), F32), pltpu.VMEM((TM, D_IN), F32)],
        compiler_params=pltpu.CompilerParams(
            dimension_semantics=("arbitrary",), vmem_limit_bytes=VMEM_LIMIT),
        name="inproj",
    )(x_all, mod, w_in_b, cos_t, sin_t, wsc, bsg, gsgu, gq2, gk2, gob)


def _split_heads(qg):
    lo = _low_half(qg.shape)
    zero = jnp.zeros_like(qg)
    return jnp.concatenate([jnp.where(lo, qg, zero), jnp.where(lo, zero, qg)], 0)


def _join_heads(o, m):
    return jnp.where(_low_half((m, LANES)), o[0:m], o[m:2 * m])


def _write_merged(o_ref, rows, outs, gout_ref):
    width = len(outs) * LANES
    ss = outs[0] * outs[0]
    for o in outs[1:]:
        ss = ss + o * o
    r = lax.rsqrt(jnp.sum(ss, -1, keepdims=True) * (1.0 / width) + LN_EPS)
    for p, o in enumerate(outs):
        cols = slice(p * LANES, (p + 1) * LANES)
        o_ref[rows, cols] = (o * r * gout_ref[:, cols]).astype(BF16)


def _na_latent(q_ref, k_ref, v_ref, kx_ref, vx_ref, bias_ref, gout_ref, o_ref, t, grid_rows):
    span = NA_UNION_ROWS * GRID_W
    first = jnp.clip(t * NA_ROWS_PER_STEP - NA_WIN_R // 2, 0, grid_rows - NA_UNION_ROWS)
    start = pl.multiple_of(first * GRID_W, GRID_W)
    outs = []
    for p in range(D_A // LANES):
        cols = slice(p * LANES, (p + 1) * LANES)
        lhs = _split_heads(q_ref[:, cols])
        s_loc = _dot_nt(lhs, k_ref[pl.ds(start, span), cols]) + bias_ref[0, p]
        s_ctx = _dot_nt(lhs, kx_ref[:, cols])
        m = jnp.maximum(jnp.max(s_loc, -1, keepdims=True), jnp.max(s_ctx, -1, keepdims=True))
        e_loc = jnp.exp(s_loc - m)
        e_ctx = jnp.exp(s_ctx - m)
        l = jnp.sum(e_loc, -1, keepdims=True) + jnp.sum(e_ctx, -1, keepdims=True)
        o = _dot(e_loc.astype(BF16), v_ref[pl.ds(start, span), cols]) \
            + _dot(e_ctx.astype(BF16), vx_ref[:, cols])
        outs.append(_join_heads(o * (1.0 / l), TQ))
    _write_merged(o_ref, slice(0, TQ), outs, gout_ref)


def _na_context(q_ref, kx_ref, vx_ref, gout_ref, o_ref):
    outs = []
    for p in range(D_A // LANES):
        cols = slice(p * LANES, (p + 1) * LANES)
        lhs = _split_heads(q_ref[:, cols])
        s = _dot_nt(lhs, kx_ref[:, cols])
        e = jnp.exp(s - jnp.max(s, -1, keepdims=True))
        o = _dot(e.astype(BF16), vx_ref[:, cols]) * (1.0 / jnp.sum(e, -1, keepdims=True))
        outs.append(_join_heads(o, TQ))
    _write_merged(o_ref, slice(0, TQ), outs, gout_ref)


def _gqa_fixed_shift(lhs_ref, k_ref, v_ref, kx_ref, vx_ref, l_ref, acc_ref, score_bound, n_k_chunks):
    def chunk(k_c, v_c):
        p = jnp.exp2(_dot_nt(lhs_ref[...], k_c) - score_bound)
        l_part = p[:, 0:LANES]
        for i in range(1, k_c.shape[0] // LANES):
            l_part = l_part + p[:, i * LANES:(i + 1) * LANES]
        return l_part, _dot(p.astype(BF16), v_c)

    l0, a0 = chunk(kx_ref[...], vx_ref[...])
    l_ref[...] = l0
    acc_ref[...] = a0
    for c in range(n_k_chunks):
        l_c, a_c = chunk(k_ref[c * TK:(c + 1) * TK, :], v_ref[c * TK:(c + 1) * TK, :])
        l_ref[...] += l_c
        acc_ref[...] += a_c
    acc_ref[...] = acc_ref[...] * (1.0 / jnp.sum(l_ref[...], -1, keepdims=True))


def _gqa_running_max(lhs_ref, k_ref, v_ref, kx_ref, vx_ref, m_ref, l_ref, acc_ref, latent, n_k_chunks):
    m_rows = GQA_Q_HEADS * TQ
    s = _dot_nt(lhs_ref[...], kx_ref[...])
    m0 = jnp.max(s, -1, keepdims=True)
    e = jnp.exp2(s - m0)
    m_ref[...] = jnp.broadcast_to(m0, (m_rows, LANES))
    l_ref[...] = jnp.broadcast_to(jnp.sum(e, -1, keepdims=True), (m_rows, LANES))
    acc_ref[...] = _dot(e.astype(BF16), vx_ref[...])

    @pl.when(latent)
    def _latent_keys():
        def body(c, carry):
            start = pl.multiple_of(c * TK, TK)
            s = _dot_nt(lhs_ref[...], k_ref[pl.ds(start, TK), :])
            m_prev = m_ref[:, 0:1]
            m_new = jnp.maximum(m_prev, jnp.max(s, -1, keepdims=True))
            alpha = jnp.exp2(m_prev - m_new)
            e = jnp.exp2(s - m_new)
            l_new = alpha * l_ref[:, 0:1] + jnp.sum(e, -1, keepdims=True)
            acc_ref[...] = alpha * acc_ref[...] + _dot(e.astype(BF16), v_ref[pl.ds(start, TK), :])
            m_ref[...] = jnp.broadcast_to(m_new, (m_rows, LANES))
            l_ref[...] = jnp.broadcast_to(l_new, (m_rows, LANES))
            return carry

        lax.fori_loop(0, n_k_chunks, body, 0)

    acc_ref[...] = acc_ref[...] * (1.0 / l_ref[:, 0:1])


def _attn_kernel(qa_ref, ka_ref, va_ref, kax_ref, vax_ref, bias_ref, gouta_ref,
                 qc_ref, kc_ref, vc_ref, kcx_ref, vcx_ref, gq_ref, gk_ref, goutc_ref,
                 oa_ref, oc_ref, lhs_ref, m_ref, l_ref, acc_ref, *, grid_rows, n_blocks, n_k_chunks):
    t = pl.program_id(1)
    latent = t < n_blocks
    n_groups = D_C // LANES

    for j in range(GQA_KV_HEADS):
        for g in range(n_groups):
            qg = qc_ref[:, g * LANES:(g + 1) * LANES]
            lo = _low_half(qg.shape)
            keep = lo if j == 0 else jnp.logical_not(lo)
            lhs_ref[(j * n_groups + g) * TQ:(j * n_groups + g + 1) * TQ, :] = jnp.where(keep, qg, jnp.zeros_like(qg))

    score_bound = (HEAD_DIM * ATTN_SCALE * LOG2_E) * jnp.max(jnp.abs(gq_ref[...])) * jnp.max(jnp.abs(gk_ref[...]))
    bounded = score_bound <= SAFE_SCORE_BOUND
    main_path = jnp.logical_and(bounded, latent)

    @pl.when(main_path)
    def _latent_bounded():
        _gqa_fixed_shift(lhs_ref, kc_ref, vc_ref, kcx_ref, vcx_ref, l_ref, acc_ref, score_bound, n_k_chunks)
        _na_latent(qa_ref, ka_ref, va_ref, kax_ref, vax_ref, bias_ref, gouta_ref, oa_ref, t, grid_rows)

    @pl.when(jnp.logical_not(main_path))
    def _other():
        @pl.when(latent)
        def _():
            _na_latent(qa_ref, ka_ref, va_ref, kax_ref, vax_ref, bias_ref, gouta_ref, oa_ref, t, grid_rows)

        @pl.when(jnp.logical_not(latent))
        def _():
            _na_context(qa_ref, kax_ref, vax_ref, gouta_ref, oa_ref)

        @pl.when(bounded)
        def _():
            _gqa_fixed_shift(lhs_ref, kc_ref, vc_ref, kcx_ref, vcx_ref, l_ref, acc_ref, score_bound, 0)

        @pl.when(jnp.logical_not(bounded))
        def _():
            _gqa_running_max(lhs_ref, kc_ref, vc_ref, kcx_ref, vcx_ref, m_ref, l_ref, acc_ref, latent, n_k_chunks)

    lo = _low_half((TQ, LANES))
    outs = [jnp.where(lo, acc_ref[g * TQ:(g + 1) * TQ, :], acc_ref[(n_groups + g) * TQ:(n_groups + g + 1) * TQ, :])
            for g in range(n_groups)]
    _write_merged(oc_ref, slice(0, TQ), outs, goutc_ref)


def _attn(qa, ka, va, bias, gout_a, qc, kc, vc, gq2, gk2, gout_c, *, layer, n_batch, seq, ctx_len, with_ctx):
    grid_rows = seq // GRID_W
    n_blocks = seq // TQ
    assert grid_rows >= NA_UNION_ROWS + NA_ROWS_PER_STEP
    ctx_block0 = n_batch * seq // ctx_len
    q_ctx_block0 = n_batch * seq // TQ
    out_rows = qa.shape[0] if with_ctx else n_batch * seq
    m_rows = GQA_Q_HEADS * TQ

    def q_map(b, t):
        return (jnp.where(t < n_blocks, b * n_blocks + t, q_ctx_block0 + b), 0)

    def batch_map(b, t):
        return (b, 0)

    def ctx_map(b, t):
        return (ctx_block0 + b, 0)

    def bias_map(b, t):
        return (layer, (t > 0).astype(jnp.int32) + (t >= n_blocks - 1).astype(jnp.int32), 0, 0, 0)

    single = pl.Buffered(1)
    return pl.pallas_call(
        functools.partial(_attn_kernel, grid_rows=grid_rows, n_blocks=n_blocks, n_k_chunks=seq // TK),
        grid=(n_batch, n_blocks + (1 if with_ctx else 0)),
        in_specs=[
            pl.BlockSpec((TQ, D_A), q_map),
            pl.BlockSpec((seq, D_A), batch_map, pipeline_mode=single),
            pl.BlockSpec((seq, D_A), batch_map, pipeline_mode=single),
            pl.BlockSpec((ctx_len, D_A), ctx_map),
            pl.BlockSpec((ctx_len, D_A), ctx_map),
            pl.BlockSpec((None, 1) + bias.shape[2:], bias_map, pipeline_mode=single),
            _layer_spec(layer, (1, D_A)),
            pl.BlockSpec((TQ, D_C), q_map),
            pl.BlockSpec((seq, D_KV_C), batch_map),
            pl.BlockSpec((seq, D_KV_C), batch_map),
            pl.BlockSpec((ctx_len, D_KV_C), ctx_map),
            pl.BlockSpec((ctx_len, D_KV_C), ctx_map),
            _layer_spec(layer, (1, LANES)),
            _layer_spec(layer, (1, LANES)),
            _layer_spec(layer, (1, D_C)),
        ],
        out_specs=[pl.BlockSpec((TQ, D_A), q_map), pl.BlockSpec((TQ, D_C), q_map)],
        out_shape=[jax.ShapeDtypeStruct((out_rows, D_A), BF16), jax.ShapeDtypeStruct((out_rows, D_C), BF16)],
        scratch_shapes=[
            pltpu.VMEM((m_rows, LANES), BF16),
            pltpu.VMEM((m_rows, LANES), F32),
            pltpu.VMEM((m_rows, LANES), F32),
            pltpu.VMEM((m_rows, LANES), F32),
        ],
        compiler_params=pltpu.CompilerParams(
            dimension_semantics=("arbitrary", "arbitrary"), vmem_limit_bytes=VMEM_LIMIT),
        name="attn",
    )(qa, ka, va, ka, va, bias, gout_a, qc, kc, vc, kc, vc, gq2, gk2, gout_c)


POST_HALVES = 2
POST_LN_PIECES = 4


def _post_kernel(oa_ref, ob_ref, oc_ref, x_ref, mod_a_ref, mod_b_ref, wo_ref, wffn_ref, wout_ref,
                 ln1g_ref, ln1b_ref, ln2g_ref, ln2b_ref, o_ref, h_ref, g_ref, y_ref, *, n_tiles):
    mods = (mod_a_ref, mod_b_ref)
    piece_rows = TM // POST_LN_PIECES
    n_chunks = D_FF // FF_CHUNK

    def mod(half, i):
        return mods[half][0, :, i * D_MODEL:(i + 1) * D_MODEL]

    def out_proj(half):
        rows = slice(half * TM, (half + 1) * TM)
        y_ref[rows, :] = (_dot(oa_ref[rows, :], wo_ref[0:D_A, :]) + _dot(ob_ref[rows, :], wo_ref[D_A:D_A + D_B, :])
                          + _dot(oc_ref[rows, :], wo_ref[D_A + D_B:, :]))

    def ln1(half, piece):
        rows = slice(half * TM + piece * piece_rows, half * TM + (piece + 1) * piece_rows)
        x1 = _layer_norm(ALPHA * x_ref[rows, :] + mod(half, 2) * y_ref[rows, :]) * ln1g_ref[...] + ln1b_ref[...]
        o_ref[rows, :] = x1
        h_ref[rows, :] = (_layer_norm(x1) * (1.0 + mod(half, 4)) + mod(half, 3)).astype(BF16)

    def ffn_chunk(half, c):
        rows = slice(half * TM, (half + 1) * TM)
        cols = slice(c * FF_CHUNK, (c + 1) * FF_CHUNK)
        gate = _dot(h_ref[rows, :], wffn_ref[:, cols])
        up = _dot(h_ref[rows, :], wffn_ref[:, D_FF + c * FF_CHUNK:D_FF + (c + 1) * FF_CHUNK])
        g_ref[rows, cols] = (_silu(gate) * up).astype(BF16)

    def ffn_out(half):
        rows = slice(half * TM, (half + 1) * TM)
        y_ref[rows, :] = _dot(g_ref[rows, :], wout_ref[...])

    def ln2(half, piece):
        rows = slice(half * TM + piece * piece_rows, half * TM + (piece + 1) * piece_rows)
        z = ALPHA * o_ref[rows, :] + mod(half, 5) * y_ref[rows, :]
        o_ref[rows, :] = _layer_norm(z) * ln2g_ref[...] + ln2b_ref[...]

    def pair_of_tiles():
        out_proj(0)
        for piece in range(POST_LN_PIECES):
            ln1(0, piece)
        out_proj(1)
        for c in range(n_chunks):
            ffn_chunk(0, c)
            if c < POST_LN_PIECES:
                ln1(1, c)
        ffn_out(0)
        for c in range(n_chunks):
            ffn_chunk(1, c)
            if c < POST_LN_PIECES:
                ln2(0, c)
        ffn_out(1)
        for piece in range(POST_LN_PIECES):
            ln2(1, piece)

    def lone_tile():
        out_proj(0)
        for piece in range(POST_LN_PIECES):
            ln1(0, piece)
        for c in range(n_chunks):
            ffn_chunk(0, c)
        ffn_out(0)
        for piece in range(POST_LN_PIECES):
            ln2(0, piece)

    if n_tiles % POST_HALVES == 0:
        pair_of_tiles()
    else:
        t = pl.program_id(0)
        pl.when(t < n_tiles // POST_HALVES)(pair_of_tiles)
        pl.when(t == n_tiles // POST_HALVES)(lone_tile)


def _post(oa, ob, oc, x_all, mod, wo, wffn, wout, ln1g, ln1b, ln2g, ln2b, *, layer, n_lat, lat_per_batch,
          n_batch, n_tiles):
    rows_per_step = POST_HALVES * TM
    n_steps = -(-n_tiles // POST_HALVES)

    def row_map(t):
        return (t, 0)

    def mod_row(tile):
        tile = jnp.minimum(tile, n_tiles - 1)
        return jnp.where(tile < n_lat, tile // lat_per_batch, n_batch)

    def resident(shape):
        return pl.BlockSpec((None,) + shape, lambda t: (layer, 0, 0), pipeline_mode=pl.Buffered(1))

    return pl.pallas_call(
        functools.partial(_post_kernel, n_tiles=n_tiles),
        grid=(n_steps,),
        in_specs=[
            pl.BlockSpec((rows_per_step, D_A), row_map),
            pl.BlockSpec((rows_per_step, D_B), row_map),
            pl.BlockSpec((rows_per_step, D_C), row_map),
            pl.BlockSpec((rows_per_step, D_MODEL), row_map),
            pl.BlockSpec((None, 1, 1, N_MOD * D_MODEL), lambda t: (layer, mod_row(POST_HALVES * t), 0, 0)),
            pl.BlockSpec((None, 1, 1, N_MOD * D_MODEL), lambda t: (layer, mod_row(POST_HALVES * t + 1), 0, 0)),
            resident((D_MODEL, D_MODEL)),
            resident((D_MODEL, 2 * D_FF)),
            resident((D_FF, D_MODEL)),
            _layer_spec(layer, (1, D_MODEL)),
            _layer_spec(layer, (1, D_MODEL)),
            _layer_spec(layer, (1, D_MODEL)),
            _layer_spec(layer, (1, D_MODEL)),
        ],
        out_specs=pl.BlockSpec((rows_per_step, D_MODEL), row_map),
        out_shape=jax.ShapeDtypeStruct((n_tiles * TM, D_MODEL), F32),
        scratch_shapes=[pltpu.VMEM((rows_per_step, D_MODEL), BF16), pltpu.VMEM((rows_per_step, D_FF), BF16),
                        pltpu.VMEM((rows_per_step, D_MODEL), F32)],
        compiler_params=pltpu.CompilerParams(
            dimension_semantics=("arbitrary",), vmem_limit_bytes=VMEM_LIMIT),
        name="post",
    )(oa, ob, oc, x_all, mod, mod, wo, wffn, wout, ln1g, ln1b, ln2g, ln2b)


def _rope_tables(seq):
    t = jnp.arange(seq, dtype=jnp.int32)
    row = (t // GRID_W).astype(F32)
    col = (t % GRID_W).astype(F32)
    n_freq = HEAD_DIM // 4
    inv = 1.0 / (ROPE_THETA ** (jnp.arange(n_freq, dtype=F32) / n_freq))
    ang = jnp.stack([row[:, None] * inv, col[:, None] * inv], axis=1)
    cos = jnp.cos(ang)[:, :, None, :]
    sin = jnp.sin(ang)[:, :, None, :]
    cos_h = jnp.broadcast_to(cos, (seq, 2, 2, n_freq)).reshape(seq, HEAD_DIM)
    sign = jnp.array([-1.0, 1.0], F32)[None, None, :, None]
    sin_h = jnp.broadcast_to(sin * sign, (seq, 2, 2, n_freq)).reshape(seq, HEAD_DIM)
    cos_t = jnp.concatenate([jnp.tile(cos_h, (1, 2)), jnp.ones((TM, LANES), F32)], 0)
    sin_t = jnp.concatenate([jnp.tile(sin_h, (1, 2)), jnp.zeros((TM, LANES), F32)], 0)
    return cos_t, sin_t


def _na_bias_tables(rpb):
    depth = rpb.shape[0]
    c_idx = np.arange(GRID_W)
    col_start = np.clip(c_idx - NA_WIN_C // 2, 0, GRID_W - NA_WIN_C)
    kc = np.arange(GRID_W)
    col_ok = (kc[None, :] >= col_start[:, None]) & (kc[None, :] < col_start[:, None] + NA_WIN_C)
    pad = GRID_W - NA_WIN_C
    padded = jnp.pad(rpb, ((0, 0), (0, 0), (0, 0), (pad, pad)))
    toep = jnp.stack([padded[..., GRID_W - 1 - c:2 * GRID_W - 1 - c] for c in range(GRID_W)], axis=-2)
    toep = jnp.where(jnp.asarray(col_ok), toep, MASK_VALUE)
    lead = np.array([0, NA_WIN_R // 2, NA_UNION_ROWS - NA_ROWS_PER_STEP])[:, None, None]
    i = np.arange(NA_ROWS_PER_STEP)[None, :, None]
    j = np.arange(NA_UNION_ROWS)[None, None, :]
    win_first = np.stack([np.zeros((NA_ROWS_PER_STEP, 1), np.int64), np.arange(NA_ROWS_PER_STEP)[:, None],
                          np.full((NA_ROWS_PER_STEP, 1), NA_UNION_ROWS - NA_WIN_R)])
    row_ok = (j >= win_first) & (j < win_first + NA_WIN_R)
    dr = j - lead - i + (NA_WIN_R - 1)
    masked = jnp.full((depth, NA_HEADS, GRID_W, GRID_W), MASK_VALUE, F32)
    variants = []
    for v in range(3):
        per_row = []
        for qi in range(NA_ROWS_PER_STEP):
            blocks = [toep[:, :, int(dr[v, qi, kj])] if row_ok[v, qi, kj] else masked
                      for kj in range(NA_UNION_ROWS)]
            per_row.append(jnp.concatenate(blocks, axis=-1))
        variants.append(jnp.concatenate(per_row, axis=-2))
    tab = jnp.stack(variants, axis=1)
    return tab.reshape(depth, 3, NA_HEADS // 2, 2 * TQ, NA_UNION_ROWS * GRID_W)


def kernel(x, c, ctx, c_ctx, w_mod, b_mod, w_in, rpb, w_s, b_s, g_sgu, g_q, g_k, g_out, w_o,
           ln1_g, ln1_b, w_ffn_in, w_ffn_out, ln2_g, ln2_b):
    n_batch, seq, _ = x.shape
    ctx_len = ctx.shape[1]
    depth = w_in.shape[0]
    assert seq % TM == 0 and (n_batch * ctx_len) % TM == 0 and ctx_len == TQ and seq % TK == 0
    assert n_batch + 1 <= 8
    n_lat = n_batch * seq // TM
    n_tiles = n_lat + n_batch * ctx_len // TM
    lat_per_batch = seq // TM

    x_all = jnp.concatenate([x.reshape(n_batch * seq, D_MODEL), ctx.reshape(n_batch * ctx_len, D_MODEL)], 0)

    c_all = jnp.zeros((8, D_MODEL), F32).at[:n_batch].set(c).at[n_batch].set(c_ctx)
    mod = _modulation(c_all, w_mod, b_mod)[:, :n_batch + 1].reshape(depth, n_batch + 1, 1, N_MOD * D_MODEL)

    def reorder_gqa_heads(a, axis, off):
        parts = [lax.slice_in_dim(a, 0, off, axis=axis)]
        parts += [lax.slice_in_dim(a, off + h * HEAD_DIM, off + (h + 1) * HEAD_DIM, axis=axis)
                  for h in GQA_HEAD_ORDER]
        parts.append(lax.slice_in_dim(a, off + D_C, a.shape[axis], axis=axis))
        return jnp.concatenate(parts, axis)

    def rows(v):
        return v.reshape(depth, 1, -1)

    w_in_b = reorder_gqa_heads(w_in, 2, OFF_QC).astype(BF16)
    w_o_b = reorder_gqa_heads(w_o, 1, D_A + D_B).astype(BF16)
    g_out_p = reorder_gqa_heads(g_out, 1, D_A + D_B)
    g_out_a, g_out_b, g_out_c = (rows(g_out_p[:, :D_A]), rows(g_out_p[:, D_A:D_A + D_B]),
                                 rows(g_out_p[:, D_A + D_B:]))
    w_ffn_b = w_ffn_in.astype(BF16)
    w_out_b = w_ffn_out.astype(BF16)
    w_s_cat = jnp.transpose(w_s, (0, 2, 1, 3)).reshape(depth, SG_CHUNK, SG_GROUPS * SG_CHUNK).astype(BF16)
    b_s_full = jnp.repeat(jnp.transpose(b_s, (0, 2, 1)), HEAD_DIM, axis=2)
    g_sgu3 = rows(g_sgu)
    g_q2 = rows(jnp.tile(g_q, (1, 2)))
    g_k2 = rows(jnp.tile(g_k, (1, 2)))
    ln1g, ln1b, ln2g, ln2b = rows(ln1_g), rows(ln1_b), rows(ln2_g), rows(ln2_b)
    cos_t, sin_t = _rope_tables(seq)
    bias = _na_bias_tables(rpb)

    for l in range(depth):
        with_ctx = l < depth - 1
        qa, ka, va, ob, qc, kc, vc = _inproj(
            x_all, mod, w_in_b, cos_t, sin_t, w_s_cat, b_s_full, g_sgu3, g_q2, g_k2, g_out_b,
            layer=l, n_lat=n_lat, lat_per_batch=lat_per_batch, n_batch=n_batch, rope_lat_tiles=seq // TM)
        oa, oc = _attn(qa, ka, va, bias, g_out_a, qc, kc, vc, g_q2, g_k2, g_out_c,
                       layer=l, n_batch=n_batch, seq=seq, ctx_len=ctx_len, with_ctx=with_ctx)
        x_all = _post(oa, ob, oc, x_all, mod, w_o_b, w_ffn_b, w_out_b, ln1g, ln1b, ln2g, ln2b,
                      layer=l, n_lat=n_lat, lat_per_batch=lat_per_batch, n_batch=n_batch,
                      n_tiles=n_tiles if with_ctx else n_lat)
    return x_all.reshape(n_batch, seq, D_MODEL)
```

```python
import functools
import math

import jax
import jax.numpy as jnp
import numpy as np
from jax import lax
from jax.experimental import pallas as pl
from jax.experimental.pallas import tpu as pltpu

F32 = jnp.float32
BF16 = jnp.bfloat16

D_MODEL = 1024
HEAD_DIM = 64
GRID_W = 64
NA_HEADS = 6
NA_WIN_R = 8
NA_WIN_C = 16
SG_GROUPS = 4
SG_CHUNK = 128
GQA_Q_HEADS = 6
GQA_KV_HEADS = 2
ROPE_THETA = 10000.0
MODEL_DEPTH = 4

D_A = NA_HEADS * HEAD_DIM
D_B = SG_GROUPS * HEAD_DIM
D_C = GQA_Q_HEADS * HEAD_DIM
D_KV_C = GQA_KV_HEADS * HEAD_DIM
D_IN = 3 * D_A + 2 * D_B + D_C + 2 * D_KV_C
D_FF = int(math.ceil(8 * D_MODEL / 3 / 256)) * 256
N_MOD = 6
ALPHA = (2 * MODEL_DEPTH) ** 0.25
LN_EPS = 1e-6
ATTN_SCALE = HEAD_DIM ** -0.5
MASK_VALUE = -1e30
LOG2_E = 1.4426950408889634
SAFE_SCORE_BOUND = 57.0

OFF_QA, OFF_KA, OFF_VA = 0, D_A, 2 * D_A
OFF_ZB = 3 * D_A
OFF_QC = OFF_ZB + 2 * D_B
OFF_KC = OFF_QC + D_C
OFF_VC = OFF_KC + D_KV_C

LANES = 128
TM = 512
TQ = 256
TK = 512
NA_ROWS_PER_STEP = TQ // GRID_W
NA_UNION_ROWS = 12
FF_CHUNK = 256
MOD_TN = 1536
VMEM_LIMIT = 56 * 1024 * 1024

GQA_HEAD_ORDER = (0, 3, 1, 4, 2, 5)


def _dot(a, b):
    return jnp.dot(a, b, preferred_element_type=F32)


def _dot_nt(a, b):
    return lax.dot_general(a, b, (((1,), (1,)), ((), ())), preferred_element_type=F32)


def _layer_norm(x):
    mu = jnp.mean(x, -1, keepdims=True)
    xc = x - mu
    var = jnp.mean(xc * xc, -1, keepdims=True)
    return xc * lax.rsqrt(var + LN_EPS)


def _rms(x):
    return x * lax.rsqrt(jnp.mean(x * x, -1, keepdims=True) + LN_EPS)


def _gelu_tanh(x):
    return x * (0.5 * (1.0 + jnp.tanh(0.7978845608028654 * (x + 0.044715 * (x * x * x)))))


def _silu(x):
    return x * (1.0 / (1.0 + jnp.exp(-x)))


def _low_half(shape):
    return lax.broadcasted_iota(jnp.int32, shape, 1) < HEAD_DIM


def _pair_rms(xg, gain):
    lo = _low_half(xg.shape)
    sq = xg * xg
    s_lo = jnp.sum(jnp.where(lo, sq, 0.0), -1, keepdims=True)
    s_hi = jnp.sum(jnp.where(lo, 0.0, sq), -1, keepdims=True)
    r = jnp.where(lo, lax.rsqrt(s_lo * (1.0 / HEAD_DIM) + LN_EPS),
                  lax.rsqrt(s_hi * (1.0 / HEAD_DIM) + LN_EPS))
    return xg * r * gain


def _rope(xn, cos_t, sin_t):
    lane = lax.broadcasted_iota(jnp.int32, xn.shape, 1)
    first = (lane % 32) < 16
    partner = jnp.where(first, pltpu.roll(xn, LANES - 16, 1), pltpu.roll(xn, 16, 1))
    return xn * cos_t + partner * sin_t


def _mod_kernel(c_ref, w_ref, b_ref, o_ref):
    sc = _silu(c_ref[...]).astype(BF16)
    o_ref[...] = _dot(sc, w_ref[...].astype(BF16)) + b_ref[...]


def _modulation(c_all, w_mod, b_mod):
    depth = w_mod.shape[0]
    n_out = w_mod.shape[2]
    return pl.pallas_call(
        _mod_kernel,
        grid=(depth, n_out // MOD_TN),
        in_specs=[
            pl.BlockSpec((8, D_MODEL), lambda l, j: (0, 0)),
            pl.BlockSpec((None, D_MODEL, MOD_TN), lambda l, j: (l, 0, j)),
            pl.BlockSpec((None, 1, MOD_TN), lambda l, j: (l, 0, j)),
        ],
        out_specs=pl.BlockSpec((None, 8, MOD_TN), lambda l, j: (l, 0, j)),
        out_shape=jax.ShapeDtypeStruct((depth, 8, n_out), F32),
        compiler_params=pltpu.CompilerParams(
            dimension_semantics=("arbitrary", "arbitrary"), vmem_limit_bytes=VMEM_LIMIT),
        name="modulation",
    )(c_all, w_mod, b_mod.reshape(depth, 1, n_out))


def _inproj_kernel(x_ref, mod_ref, w_ref, cos_ref, sin_ref, wsc_ref, bsg_ref, gsgu_ref,
                   gq_ref, gk_ref, gob_ref,
                   qa_ref, ka_ref, va_ref, ob_ref, qc_ref, kc_ref, vc_ref, y_even_ref, y_odd_ref):
    t = pl.program_id(0)

    @pl.when(t == 0)
    def _no_previous_tile():
        y_odd_ref[...] = jnp.zeros_like(y_odd_ref)

    @pl.when(t % 2 == 0)
    def _even():
        _inproj_step(x_ref, mod_ref, w_ref, cos_ref, sin_ref, wsc_ref, bsg_ref, gsgu_ref, gq_ref, gk_ref, gob_ref,
                     qa_ref, ka_ref, va_ref, ob_ref, qc_ref, kc_ref, vc_ref, y_even_ref, y_odd_ref)

    @pl.when(t % 2 == 1)
    def _odd():
        _inproj_step(x_ref, mod_ref, w_ref, cos_ref, sin_ref, wsc_ref, bsg_ref, gsgu_ref, gq_ref, gk_ref, gob_ref,
                     qa_ref, ka_ref, va_ref, ob_ref, qc_ref, kc_ref, vc_ref, y_odd_ref, y_even_ref)


def _inproj_step(x_ref, mod_ref, w_ref, cos_ref, sin_ref, wsc_ref, bsg_ref, gsgu_ref, gq_ref, gk_ref, gob_ref,
                 qa_ref, ka_ref, va_ref, ob_ref, qc_ref, kc_ref, vc_ref, y_new_ref, y_ref):
    shift = mod_ref[0, :, 0:D_MODEL]
    scale = mod_ref[0, :, D_MODEL:2 * D_MODEL]
    h = (_layer_norm(x_ref[...]) * (1.0 + scale) + shift).astype(BF16)
    y_new_ref[...] = _dot(h, w_ref[...])

    def proj(off, width):
        return y_ref[:, off:off + width]

    qa_ref[...] = (proj(OFF_QA, D_A) * ATTN_SCALE).astype(BF16)
    ka_ref[...] = proj(OFF_KA, D_A).astype(BF16)
    va_ref[...] = proj(OFF_VA, D_A).astype(BF16)

    u = _gelu_tanh(proj(OFF_ZB, D_B))
    v = (_layer_norm(_gelu_tanh(proj(OFF_ZB + D_B, D_B))) * gsgu_ref[...]).astype(BF16)
    lane_group = lax.broadcasted_iota(jnp.int32, (SG_CHUNK, D_B), 1) // HEAD_DIM
    for c in range(TM // SG_CHUNK):
        rows = slice(c * SG_CHUNK, (c + 1) * SG_CHUNK)
        vch = v[rows]
        rhs = jnp.concatenate(
            [jnp.where(lane_group == g, vch, jnp.zeros_like(vch)) for g in range(SG_GROUPS)], 0)
        mixed = _dot(wsc_ref[...], rhs) + bsg_ref[...]
        ob_ref[rows, :] = (_rms(u[rows] * mixed) * gob_ref[...]).astype(BF16)

    cos_t = cos_ref[...]
    sin_t = sin_ref[...]
    for p in range(D_C // LANES):
        xq = proj(OFF_QC + p * LANES, LANES)
        xq = _rope(_pair_rms(xq, gq_ref[...]), cos_t, sin_t)
        qc_ref[:, p * LANES:(p + 1) * LANES] = (xq * (ATTN_SCALE * LOG2_E)).astype(BF16)
    xk = _rope(_pair_rms(proj(OFF_KC, D_KV_C), gk_ref[...]), cos_t, sin_t)
    kc_ref[...] = xk.astype(BF16)
    vc_ref[...] = proj(OFF_VC, D_KV_C).astype(BF16)


def _layer_spec(layer, shape):
    zeros = (0,) * len(shape)
    return pl.BlockSpec((None,) + tuple(shape), lambda *_: (layer,) + zeros)


def _inproj(x_all, mod, w_in_b, cos_t, sin_t, wsc, bsg, gsgu, gq2, gk2, gob, *, layer, n_lat, lat_per_batch,
            n_batch, rope_lat_tiles):
    t_rows = x_all.shape[0]
    n_tiles = t_rows // TM

    def in_map(t):
        return (jnp.minimum(t, n_tiles - 1), 0)

    def row_map(t):
        return (jnp.maximum(t - 1, 0), 0)

    def mod_map(t):
        tile = jnp.minimum(t, n_tiles - 1)
        return (layer, jnp.where(tile < n_lat, tile // lat_per_batch, n_batch), 0, 0)

    def rope_map(t):
        tile = jnp.maximum(t - 1, 0)
        return (jnp.where(tile < n_lat, tile % rope_lat_tiles, rope_lat_tiles), 0)

    widths = (D_A, D_A, D_A, D_B, D_C, D_KV_C, D_KV_C)
    return pl.pallas_call(
        _inproj_kernel,
        grid=(n_tiles + 1,),
        in_specs=[
            pl.BlockSpec((TM, D_MODEL), in_map),
            pl.BlockSpec((None, 1, 1, N_MOD * D_MODEL), mod_map),
            _layer_spec(layer, (D_MODEL, D_IN)),
            pl.BlockSpec((TM, LANES), rope_map),
            pl.BlockSpec((TM, LANES), rope_map),
            _layer_spec(layer, (SG_CHUNK, SG_GROUPS * SG_CHUNK)),
            _layer_spec(layer, (SG_CHUNK, D_B)),
            _layer_spec(layer, (1, D_B)),
            _layer_spec(layer, (1, LANES)),
            _layer_spec(layer, (1, LANES)),
            _layer_spec(layer, (1, D_B)),
        ],
        out_specs=[pl.BlockSpec((TM, w), row_map) for w in widths],
        out_shape=[jax.ShapeDtypeStruct((t_rows, w), BF16) for w in widths],
        scratch_shapes=[pltpu.VMEM((TM, D_IN), F32), pltpu.VMEM((TM, D_IN), F32)],
        compiler_params=pltpu.CompilerParams(
            dimension_semantics=("arbitrary",), vmem_limit_bytes=VMEM_LIMIT),
        name="inproj",
    )(x_all, mod, w_in_b, cos_t, sin_t, wsc, bsg, gsgu, gq2, gk2, gob)


def _split_heads(qg):
    lo = _low_half(qg.shape)
    zero = jnp.zeros_like(qg)
    return jnp.concatenate([jnp.where(lo, qg, zero), jnp.where(lo, zero, qg)], 0)


def _join_heads(o, m):
    return jnp.where(_low_half((m, LANES)), o[0:m], o[m:2 * m])


def _write_merged(o_ref, rows, outs, gout_ref):
    width = len(outs) * LANES
    ss = outs[0] * outs[0]
    for o in outs[1:]:
        ss = ss + o * o
    r = lax.rsqrt(jnp.sum(ss, -1, keepdims=True) * (1.0 / width) + LN_EPS)
    for p, o in enumerate(outs):
        cols = slice(p * LANES, (p + 1) * LANES)
        o_ref[rows, cols] = (o * r * gout_ref[:, cols]).astype(BF16)


def _na_latent(q_ref, k_ref, v_ref, kx_ref, vx_ref, bias_ref, gout_ref, o_ref, t, grid_rows):
    span = NA_UNION_ROWS * GRID_W
    first = jnp.clip(t * NA_ROWS_PER_STEP - NA_WIN_R // 2, 0, grid_rows - NA_UNION_ROWS)
    start = pl.multiple_of(first * GRID_W, GRID_W)
    outs = []
    for p in range(D_A // LANES):
        cols = slice(p * LANES, (p + 1) * LANES)
        lhs = _split_heads(q_ref[:, cols])
        bias = jnp.concatenate([bias_ref[p, 0], bias_ref[p, 1]], 0)
        s_loc = _dot_nt(lhs, k_ref[pl.ds(start, span), cols]) + bias
        s_ctx = _dot_nt(lhs, kx_ref[:, cols])
        m = jnp.maximum(jnp.max(s_loc, -1, keepdims=True), jnp.max(s_ctx, -1, keepdims=True))
        e_loc = jnp.exp(s_loc - m)
        e_ctx = jnp.exp(s_ctx - m)
        l = jnp.sum(e_loc, -1, keepdims=True) + jnp.sum(e_ctx, -1, keepdims=True)
        o = _dot(e_loc.astype(BF16), v_ref[pl.ds(start, span), cols]) \
            + _dot(e_ctx.astype(BF16), vx_ref[:, cols])
        outs.append(_join_heads(o * (1.0 / l), TQ))
    _write_merged(o_ref, slice(0, TQ), outs, gout_ref)


def _na_context(q_ref, kx_ref, vx_ref, gout_ref, o_ref):
    outs = []
    for p in range(D_A // LANES):
        cols = slice(p * LANES, (p + 1) * LANES)
        lhs = _split_heads(q_ref[:, cols])
        s = _dot_nt(lhs, kx_ref[:, cols])
        e = jnp.exp(s - jnp.max(s, -1, keepdims=True))
        o = _dot(e.astype(BF16), vx_ref[:, cols]) * (1.0 / jnp.sum(e, -1, keepdims=True))
        outs.append(_join_heads(o, TQ))
    _write_merged(o_ref, slice(0, TQ), outs, gout_ref)


def _gqa_fixed_shift(lhs_ref, k_ref, v_ref, kx_ref, vx_ref, l_ref, acc_ref, score_bound, n_k_chunks):
    def chunk(k_c, v_c):
        p = jnp.exp2(_dot_nt(lhs_ref[...], k_c) - score_bound)
        l_part = p[:, 0:LANES]
        for i in range(1, k_c.shape[0] // LANES):
            l_part = l_part + p[:, i * LANES:(i + 1) * LANES]
        return l_part, _dot(p.astype(BF16), v_c)

    l0, a0 = chunk(kx_ref[...], vx_ref[...])
    l_ref[...] = l0
    acc_ref[...] = a0
    for c in range(n_k_chunks):
        l_c, a_c = chunk(k_ref[c * TK:(c + 1) * TK, :], v_ref[c * TK:(c + 1) * TK, :])
        l_ref[...] += l_c
        acc_ref[...] += a_c
    acc_ref[...] = acc_ref[...] * (1.0 / jnp.sum(l_ref[...], -1, keepdims=True))


def _gqa_running_max(lhs_ref, k_ref, v_ref, kx_ref, vx_ref, m_ref, l_ref, acc_ref, latent, n_k_chunks):
    m_rows = GQA_Q_HEADS * TQ
    s = _dot_nt(lhs_ref[...], kx_ref[...])
    m0 = jnp.max(s, -1, keepdims=True)
    e = jnp.exp2(s - m0)
    m_ref[...] = jnp.broadcast_to(m0, (m_rows, LANES))
    l_ref[...] = jnp.broadcast_to(jnp.sum(e, -1, keepdims=True), (m_rows, LANES))
    acc_ref[...] = _dot(e.astype(BF16), vx_ref[...])

    @pl.when(latent)
    def _latent_keys():
        def body(c, carry):
            start = pl.multiple_of(c * TK, TK)
            s = _dot_nt(lhs_ref[...], k_ref[pl.ds(start, TK), :])
            m_prev = m_ref[:, 0:1]
            m_new = jnp.maximum(m_prev, jnp.max(s, -1, keepdims=True))
            alpha = jnp.exp2(m_prev - m_new)
            e = jnp.exp2(s - m_new)
            l_new = alpha * l_ref[:, 0:1] + jnp.sum(e, -1, keepdims=True)
            acc_ref[...] = alpha * acc_ref[...] + _dot(e.astype(BF16), v_ref[pl.ds(start, TK), :])
            m_ref[...] = jnp.broadcast_to(m_new, (m_rows, LANES))
            l_ref[...] = jnp.broadcast_to(l_new, (m_rows, LANES))
            return carry

        lax.fori_loop(0, n_k_chunks, body, 0)

    acc_ref[...] = acc_ref[...] * (1.0 / l_ref[:, 0:1])


def _attn_kernel(qa_ref, ka_ref, va_ref, kax_ref, vax_ref, bias_ref, gouta_ref,
                 qc_ref, kc_ref, vc_ref, kcx_ref, vcx_ref, gq_ref, gk_ref, goutc_ref,
                 oa_ref, oc_ref, lhs_ref, m_ref, l_ref, acc_ref, *, grid_rows, n_blocks, n_k_chunks):
    t = pl.program_id(1)
    latent = t < n_blocks
    n_groups = D_C // LANES

    for j in range(GQA_KV_HEADS):
        for g in range(n_groups):
            qg = qc_ref[:, g * LANES:(g + 1) * LANES]
            lo = _low_half(qg.shape)
            keep = lo if j == 0 else jnp.logical_not(lo)
            lhs_ref[(j * n_groups + g) * TQ:(j * n_groups + g + 1) * TQ, :] = jnp.where(keep, qg, jnp.zeros_like(qg))

    score_bound = (HEAD_DIM * ATTN_SCALE * LOG2_E) * jnp.max(jnp.abs(gq_ref[...])) * jnp.max(jnp.abs(gk_ref[...]))
    bounded = score_bound <= SAFE_SCORE_BOUND
    main_path = jnp.logical_and(bounded, latent)

    @pl.when(main_path)
    def _latent_bounded():
        _gqa_fixed_shift(lhs_ref, kc_ref, vc_ref, kcx_ref, vcx_ref, l_ref, acc_ref, score_bound, n_k_chunks)
        _na_latent(qa_ref, ka_ref, va_ref, kax_ref, vax_ref, bias_ref, gouta_ref, oa_ref, t, grid_rows)

    @pl.when(jnp.logical_not(main_path))
    def _other():
        @pl.when(latent)
        def _():
            _na_latent(qa_ref, ka_ref, va_ref, kax_ref, vax_ref, bias_ref, gouta_ref, oa_ref, t, grid_rows)

        @pl.when(jnp.logical_not(latent))
        def _():
            _na_context(qa_ref, kax_ref, vax_ref, gouta_ref, oa_ref)

        @pl.when(bounded)
        def _():
            _gqa_fixed_shift(lhs_ref, kc_ref, vc_ref, kcx_ref, vcx_ref, l_ref, acc_ref, score_bound, 0)

        @pl.when(jnp.logical_not(bounded))
        def _():
            _gqa_running_max(lhs_ref, kc_ref, vc_ref, kcx_ref, vcx_ref, m_ref, l_ref, acc_ref, latent, n_k_chunks)

    lo = _low_half((TQ, LANES))
    outs = [jnp.where(lo, acc_ref[g * TQ:(g + 1) * TQ, :], acc_ref[(n_groups + g) * TQ:(n_groups + g + 1) * TQ, :])
            for g in range(n_groups)]
    _write_merged(oc_ref, slice(0, TQ), outs, goutc_ref)


def _attn(qa, ka, va, bias, gout_a, qc, kc, vc, gq2, gk2, gout_c, *, layer, n_batch, seq, ctx_len, with_ctx):
    grid_rows = seq // GRID_W
    n_blocks = seq // TQ
    assert grid_rows >= NA_UNION_ROWS + NA_ROWS_PER_STEP
    ctx_block0 = n_batch * seq // ctx_len
    q_ctx_block0 = n_batch * seq // TQ
    out_rows = qa.shape[0] if with_ctx else n_batch * seq
    m_rows = GQA_Q_HEADS * TQ

    def q_map(b, t):
        return (jnp.where(t < n_blocks, b * n_blocks + t, q_ctx_block0 + b), 0)

    def batch_map(b, t):
        return (b, 0)

    def ctx_map(b, t):
        return (ctx_block0 + b, 0)

    def bias_map(b, t):
        return (layer, 0, 0, (t > 0).astype(jnp.int32) + (t >= n_blocks - 1).astype(jnp.int32), 0, 0)

    single = pl.Buffered(1)
    return pl.pallas_call(
        functools.partial(_attn_kernel, grid_rows=grid_rows, n_blocks=n_blocks, n_k_chunks=seq // TK),
        grid=(n_batch, n_blocks + (1 if with_ctx else 0)),
        in_specs=[
            pl.BlockSpec((TQ, D_A), q_map),
            pl.BlockSpec((seq, D_A), batch_map, pipeline_mode=single),
            pl.BlockSpec((seq, D_A), batch_map, pipeline_mode=single),
            pl.BlockSpec((ctx_len, D_A), ctx_map),
            pl.BlockSpec((ctx_len, D_A), ctx_map),
            pl.BlockSpec((None,) + bias.shape[1:3] + (None,) + bias.shape[4:], bias_map, pipeline_mode=single),
            _layer_spec(layer, (1, D_A)),
            pl.BlockSpec((TQ, D_C), q_map),
            pl.BlockSpec((seq, D_KV_C), batch_map),
            pl.BlockSpec((seq, D_KV_C), batch_map),
            pl.BlockSpec((ctx_len, D_KV_C), ctx_map),
            pl.BlockSpec((ctx_len, D_KV_C), ctx_map),
            _layer_spec(layer, (1, LANES)),
            _layer_spec(layer, (1, LANES)),
            _layer_spec(layer, (1, D_C)),
        ],
        out_specs=[pl.BlockSpec((TQ, D_A), q_map), pl.BlockSpec((TQ, D_C), q_map)],
        out_shape=[jax.ShapeDtypeStruct((out_rows, D_A), BF16), jax.ShapeDtypeStruct((out_rows, D_C), BF16)],
        scratch_shapes=[
            pltpu.VMEM((m_rows, LANES), BF16),
            pltpu.VMEM((m_rows, LANES), F32),
            pltpu.VMEM((m_rows, LANES), F32),
            pltpu.VMEM((m_rows, LANES), F32),
        ],
        compiler_params=pltpu.CompilerParams(
            dimension_semantics=("arbitrary", "arbitrary"), vmem_limit_bytes=VMEM_LIMIT),
        name="attn",
    )(qa, ka, va, ka, va, bias, gout_a, qc, kc, vc, kc, vc, gq2, gk2, gout_c)


def _post_kernel(oa_ref, ob_ref, oc_ref, x_ref, mod_ref, wo_ref, wffn_ref, wout_ref,
                 ln1g_ref, ln1b_ref, ln2g_ref, ln2b_ref, o_ref, h_ref, g_ref):
    def mod(i):
        return mod_ref[0, :, i * D_MODEL:(i + 1) * D_MODEL]

    y = (_dot(oa_ref[...], wo_ref[0:D_A, :]) + _dot(ob_ref[...], wo_ref[D_A:D_A + D_B, :])
         + _dot(oc_ref[...], wo_ref[D_A + D_B:, :]))
    x1 = _layer_norm(ALPHA * x_ref[...] + mod(2) * y) * ln1g_ref[...] + ln1b_ref[...]
    h_ref[...] = (_layer_norm(x1) * (1.0 + mod(4)) + mod(3)).astype(BF16)
    for c in range(D_FF // FF_CHUNK):
        cols = slice(c * FF_CHUNK, (c + 1) * FF_CHUNK)
        gate = _dot(h_ref[...], wffn_ref[:, cols])
        up = _dot(h_ref[...], wffn_ref[:, D_FF + c * FF_CHUNK:D_FF + (c + 1) * FF_CHUNK])
        g_ref[:, cols] = (_silu(gate) * up).astype(BF16)
    ff = _dot(g_ref[...], wout_ref[...])
    o_ref[...] = _layer_norm(ALPHA * x1 + mod(5) * ff) * ln2g_ref[...] + ln2b_ref[...]


def _post(oa, ob, oc, x_all, mod, wo, wffn, wout, ln1g, ln1b, ln2g, ln2b, *, layer, n_lat, lat_per_batch,
          n_batch, n_tiles):
    def row_map(t):
        return (t, 0)

    def mod_map(t):
        return (layer, jnp.where(t < n_lat, t // lat_per_batch, n_batch), 0, 0)

    def resident(shape):
        return pl.BlockSpec((None,) + shape, lambda t: (layer, 0, 0), pipeline_mode=pl.Buffered(1))

    return pl.pallas_call(
        _post_kernel,
        grid=(n_tiles,),
        in_specs=[
            pl.BlockSpec((TM, D_A), row_map),
            pl.BlockSpec((TM, D_B), row_map),
            pl.BlockSpec((TM, D_C), row_map),
            pl.BlockSpec((TM, D_MODEL), row_map),
            pl.BlockSpec((None, 1, 1, N_MOD * D_MODEL), mod_map),
            resident((D_MODEL, D_MODEL)),
            resident((D_MODEL, 2 * D_FF)),
            resident((D_FF, D_MODEL)),
            _layer_spec(layer, (1, D_MODEL)),
            _layer_spec(layer, (1, D_MODEL)),
            _layer_spec(layer, (1, D_MODEL)),
            _layer_spec(layer, (1, D_MODEL)),
        ],
        out_specs=pl.BlockSpec((TM, D_MODEL), row_map),
        out_shape=jax.ShapeDtypeStruct((n_tiles * TM, D_MODEL), F32),
        scratch_shapes=[pltpu.VMEM((TM, D_MODEL), BF16), pltpu.VMEM((TM, D_FF), BF16)],
        compiler_params=pltpu.CompilerParams(
            dimension_semantics=("arbitrary",), vmem_limit_bytes=VMEM_LIMIT),
        name="post",
    )(oa, ob, oc, x_all, mod, wo, wffn, wout, ln1g, ln1b, ln2g, ln2b)


def _rope_tables(seq):
    t = jnp.arange(seq, dtype=jnp.int32)
    row = (t // GRID_W).astype(F32)
    col = (t % GRID_W).astype(F32)
    n_freq = HEAD_DIM // 4
    inv = 1.0 / (ROPE_THETA ** (jnp.arange(n_freq, dtype=F32) / n_freq))
    ang = jnp.stack([row[:, None] * inv, col[:, None] * inv], axis=1)
    cos = jnp.cos(ang)[:, :, None, :]
    sin = jnp.sin(ang)[:, :, None, :]
    cos_h = jnp.broadcast_to(cos, (seq, 2, 2, n_freq)).reshape(seq, HEAD_DIM)
    sign = jnp.array([-1.0, 1.0], F32)[None, None, :, None]
    sin_h = jnp.broadcast_to(sin * sign, (seq, 2, 2, n_freq)).reshape(seq, HEAD_DIM)
    cos_t = jnp.concatenate([jnp.tile(cos_h, (1, 2)), jnp.ones((TM, LANES), F32)], 0)
    sin_t = jnp.concatenate([jnp.tile(sin_h, (1, 2)), jnp.zeros((TM, LANES), F32)], 0)
    return cos_t, sin_t


def _na_bias_tables(rpb):
    depth = rpb.shape[0]
    c_idx = np.arange(GRID_W)
    col_start = np.clip(c_idx - NA_WIN_C // 2, 0, GRID_W - NA_WIN_C)
    kc = np.arange(GRID_W)
    col_ok = (kc[None, :] >= col_start[:, None]) & (kc[None, :] < col_start[:, None] + NA_WIN_C)
    pad = GRID_W - NA_WIN_C
    padded = jnp.pad(rpb, ((0, 0), (0, 0), (0, 0), (pad, pad)))
    toep = jnp.stack([padded[..., GRID_W - 1 - c:2 * GRID_W - 1 - c] for c in range(GRID_W)], axis=-2)
    toep = jnp.where(jnp.asarray(col_ok), toep, MASK_VALUE)
    lead = np.array([0, NA_WIN_R // 2, NA_UNION_ROWS - NA_ROWS_PER_STEP])[:, None, None]
    i = np.arange(NA_ROWS_PER_STEP)[None, :, None]
    j = np.arange(NA_UNION_ROWS)[None, None, :]
    win_first = np.stack([np.zeros((NA_ROWS_PER_STEP, 1), np.int64), np.arange(NA_ROWS_PER_STEP)[:, None],
                          np.full((NA_ROWS_PER_STEP, 1), NA_UNION_ROWS - NA_WIN_R)])
    row_ok = (j >= win_first) & (j < win_first + NA_WIN_R)
    dr = j - lead - i + (NA_WIN_R - 1)
    masked = jnp.full((depth, NA_HEADS, GRID_W, GRID_W), MASK_VALUE, F32)
    per_row = []
    for v in range(3):
        for qi in range(NA_ROWS_PER_STEP):
            blocks = [toep[:, :, int(dr[v, qi, kj])] if row_ok[v, qi, kj] else masked
                      for kj in range(NA_UNION_ROWS)]
            per_row.append(jnp.concatenate(blocks, axis=-1))
    tab = jnp.concatenate(per_row, axis=-2)
    return tab.reshape(depth, NA_HEADS // 2, 2, 3, TQ, NA_UNION_ROWS * GRID_W)


def kernel(x, c, ctx, c_ctx, w_mod, b_mod, w_in, rpb, w_s, b_s, g_sgu, g_q, g_k, g_out, w_o,
           ln1_g, ln1_b, w_ffn_in, w_ffn_out, ln2_g, ln2_b):
    n_batch, seq, _ = x.shape
    ctx_len = ctx.shape[1]
    depth = w_in.shape[0]
    assert seq % TM == 0 and (n_batch * ctx_len) % TM == 0 and ctx_len == TQ and seq % TK == 0
    assert n_batch + 1 <= 8
    n_lat = n_batch * seq // TM
    n_tiles = n_lat + n_batch * ctx_len // TM
    lat_per_batch = seq // TM

    x_all = jnp.concatenate([x.reshape(n_batch * seq, D_MODEL), ctx.reshape(n_batch * ctx_len, D_MODEL)], 0)

    c_all = jnp.zeros((8, D_MODEL), F32).at[:n_batch].set(c).at[n_batch].set(c_ctx)
    mod = _modulation(c_all, w_mod, b_mod)[:, :n_batch + 1].reshape(depth, n_batch + 1, 1, N_MOD * D_MODEL)

    def reorder_gqa_heads(a, axis, off):
        parts = [lax.slice_in_dim(a, 0, off, axis=axis)]
        parts += [lax.slice_in_dim(a, off + h * HEAD_DIM, off + (h + 1) * HEAD_DIM, axis=axis)
                  for h in GQA_HEAD_ORDER]
        parts.append(lax.slice_in_dim(a, off + D_C, a.shape[axis], axis=axis))
        return jnp.concatenate(parts, axis)

    def rows(v):
        return v.reshape(depth, 1, -1)

    w_in_b = reorder_gqa_heads(w_in, 2, OFF_QC).astype(BF16)
    w_o_b = reorder_gqa_heads(w_o, 1, D_A + D_B).astype(BF16)
    g_out_p = reorder_gqa_heads(g_out, 1, D_A + D_B)
    g_out_a, g_out_b, g_out_c = (rows(g_out_p[:, :D_A]), rows(g_out_p[:, D_A:D_A + D_B]),
                                 rows(g_out_p[:, D_A + D_B:]))
    w_ffn_b = w_ffn_in.astype(BF16)
    w_out_b = w_ffn_out.astype(BF16)
    w_s_cat = jnp.transpose(w_s, (0, 2, 1, 3)).reshape(depth, SG_CHUNK, SG_GROUPS * SG_CHUNK).astype(BF16)
    b_s_full = jnp.repeat(jnp.transpose(b_s, (0, 2, 1)), HEAD_DIM, axis=2)
    g_sgu3 = rows(g_sgu)
    g_q2 = rows(jnp.tile(g_q, (1, 2)))
    g_k2 = rows(jnp.tile(g_k, (1, 2)))
    ln1g, ln1b, ln2g, ln2b = rows(ln1_g), rows(ln1_b), rows(ln2_g), rows(ln2_b)
    cos_t, sin_t = _rope_tables(seq)
    bias = _na_bias_tables(rpb)

    for l in range(depth):
        with_ctx = l < depth - 1
        qa, ka, va, ob, qc, kc, vc = _inproj(
            x_all, mod, w_in_b, cos_t, sin_t, w_s_cat, b_s_full, g_sgu3, g_q2, g_k2, g_out_b,
            layer=l, n_lat=n_lat, lat_per_batch=lat_per_batch, n_batch=n_batch, rope_lat_tiles=seq // TM)
        oa, oc = _attn(qa, ka, va, bias, g_out_a, qc, kc, vc, g_q2, g_k2, g_out_c,
                       layer=l, n_batch=n_batch, seq=seq, ctx_len=ctx_len, with_ctx=with_ctx)
        x_all = _post(oa, ob, oc, x_all, mod, w_o_b, w_ffn_b, w_out_b, ln1g, ln1b, ln2g, ln2b,
                      layer=l, n_lat=n_lat, lat_per_batch=lat_per_batch, n_batch=n_batch,
                      n_tiles=n_tiles if with_ctx else n_lat)
    return x_all.reshape(n_batch, seq, D_MODEL)
```

```python
import functools
import math

import jax
import jax.numpy as jnp
import numpy as np
from jax import lax
from jax.experimental import pallas as pl
from jax.experimental.pallas import tpu as pltpu

F32 = jnp.float32
BF16 = jnp.bfloat16

D_MODEL = 1024
HEAD_DIM = 64
GRID_W = 64
NA_HEADS = 6
NA_WIN_R = 8
NA_WIN_C = 16
SG_GROUPS = 4
SG_CHUNK = 128
GQA_Q_HEADS = 6
GQA_KV_HEADS = 2
ROPE_THETA = 10000.0
MODEL_DEPTH = 4

D_A = NA_HEADS * HEAD_DIM
D_B = SG_GROUPS * HEAD_DIM
D_C = GQA_Q_HEADS * HEAD_DIM
D_KV_C = GQA_KV_HEADS * HEAD_DIM
D_IN = 3 * D_A + 2 * D_B + D_C + 2 * D_KV_C
D_FF = int(math.ceil(8 * D_MODEL / 3 / 256)) * 256
N_MOD = 6
ALPHA = (2 * MODEL_DEPTH) ** 0.25
LN_EPS = 1e-6
ATTN_SCALE = HEAD_DIM ** -0.5
MASK_VALUE = -1e30
LOG2_E = 1.4426950408889634
SAFE_SCORE_BOUND = 57.0

OFF_QA, OFF_KA, OFF_VA = 0, D_A, 2 * D_A
OFF_ZB = 3 * D_A
OFF_QC = OFF_ZB + 2 * D_B
OFF_KC = OFF_QC + D_C
OFF_VC = OFF_KC + D_KV_C

LANES = 128
TM = 512
TQ = 256
TK = 512
NA_ROWS_PER_STEP = TQ // GRID_W
NA_UNION_ROWS = 12
FF_CHUNK = 256
MOD_TN = 1536
VMEM_LIMIT = 56 * 1024 * 1024

GQA_HEAD_ORDER = (0, 3, 1, 4, 2, 5)


def _dot(a, b):
    return jnp.dot(a, b, preferred_element_type=F32)


def _dot_nt(a, b):
    return lax.dot_general(a, b, (((1,), (1,)), ((), ())), preferred_element_type=F32)


def _layer_norm(x):
    mu = jnp.mean(x, -1, keepdims=True)
    xc = x - mu
    var = jnp.mean(xc * xc, -1, keepdims=True)
    return xc * lax.rsqrt(var + LN_EPS)


def _rms(x):
    return x * lax.rsqrt(jnp.mean(x * x, -1, keepdims=True) + LN_EPS)


def _gelu_tanh(x):
    return x * (0.5 * (1.0 + jnp.tanh(0.7978845608028654 * (x + 0.044715 * (x * x * x)))))


def _silu(x):
    return x * (1.0 / (1.0 + jnp.exp(-x)))


def _low_half(shape):
    return lax.broadcasted_iota(jnp.int32, shape, 1) < HEAD_DIM


def _pair_rms(xg, gain):
    lo = _low_half(xg.shape)
    sq = xg * xg
    s_lo = jnp.sum(jnp.where(lo, sq, 0.0), -1, keepdims=True)
    s_hi = jnp.sum(jnp.where(lo, 0.0, sq), -1, keepdims=True)
    r = jnp.where(lo, lax.rsqrt(s_lo * (1.0 / HEAD_DIM) + LN_EPS),
                  lax.rsqrt(s_hi * (1.0 / HEAD_DIM) + LN_EPS))
    return xg * r * gain


def _rope(xn, cos_t, sin_t):
    lane = lax.broadcasted_iota(jnp.int32, xn.shape, 1)
    first = (lane % 32) < 16
    partner = jnp.where(first, pltpu.roll(xn, LANES - 16, 1), pltpu.roll(xn, 16, 1))
    return xn * cos_t + partner * sin_t


def _mod_kernel(c_ref, w_ref, b_ref, o_ref):
    sc = _silu(c_ref[...]).astype(BF16)
    o_ref[...] = _dot(sc, w_ref[...].astype(BF16)) + b_ref[...]


def _modulation(c_all, w_mod, b_mod):
    depth = w_mod.shape[0]
    n_out = w_mod.shape[2]
    return pl.pallas_call(
        _mod_kernel,
        grid=(depth, n_out // MOD_TN),
        in_specs=[
            pl.BlockSpec((8, D_MODEL), lambda l, j: (0, 0)),
            pl.BlockSpec((None, D_MODEL, MOD_TN), lambda l, j: (l, 0, j)),
            pl.BlockSpec((None, 1, MOD_TN), lambda l, j: (l, 0, j)),
        ],
        out_specs=pl.BlockSpec((None, 8, MOD_TN), lambda l, j: (l, 0, j)),
        out_shape=jax.ShapeDtypeStruct((depth, 8, n_out), F32),
        compiler_params=pltpu.CompilerParams(
            dimension_semantics=("arbitrary", "arbitrary"), vmem_limit_bytes=VMEM_LIMIT),
        name="modulation",
    )(c_all, w_mod, b_mod.reshape(depth, 1, n_out))


def _stream_tile(tile, xa_ref, xb_ref, n_a_tiles, n_tiles):
    if n_a_tiles >= n_tiles:
        return xa_ref[...]
    return jnp.where(tile < n_a_tiles, xa_ref[...], xb_ref[...])


def _stream_specs(tile_of_step, n_a_tiles, n_b_tiles):
    return [pl.BlockSpec((TM, D_MODEL), lambda t: (jnp.minimum(tile_of_step(t), n_a_tiles - 1), 0)),
            pl.BlockSpec((TM, D_MODEL), lambda t: (jnp.clip(tile_of_step(t) - n_a_tiles, 0, n_b_tiles - 1), 0))]


def _inproj_kernel(xa_ref, xb_ref, mod_ref, w_ref, cos_ref, sin_ref, wsc_ref, bsg_ref, gsgu_ref,
                   gq_ref, gk_ref, gob_ref,
                   qa_ref, ka_ref, va_ref, ob_ref, qc_ref, kc_ref, vc_ref, y_even_ref, y_odd_ref, *,
                   n_a_tiles, n_tiles):
    t = pl.program_id(0)
    x_tile = functools.partial(_stream_tile, jnp.minimum(t, n_tiles - 1), xa_ref, xb_ref, n_a_tiles, n_tiles)

    @pl.when(t == 0)
    def _no_previous_tile():
        y_odd_ref[...] = jnp.zeros_like(y_odd_ref)

    @pl.when(t % 2 == 0)
    def _even():
        _inproj_step(x_tile, mod_ref, w_ref, cos_ref, sin_ref, wsc_ref, bsg_ref, gsgu_ref, gq_ref, gk_ref, gob_ref,
                     qa_ref, ka_ref, va_ref, ob_ref, qc_ref, kc_ref, vc_ref, y_even_ref, y_odd_ref)

    @pl.when(t % 2 == 1)
    def _odd():
        _inproj_step(x_tile, mod_ref, w_ref, cos_ref, sin_ref, wsc_ref, bsg_ref, gsgu_ref, gq_ref, gk_ref, gob_ref,
                     qa_ref, ka_ref, va_ref, ob_ref, qc_ref, kc_ref, vc_ref, y_odd_ref, y_even_ref)


def _inproj_step(x_tile, mod_ref, w_ref, cos_ref, sin_ref, wsc_ref, bsg_ref, gsgu_ref, gq_ref, gk_ref, gob_ref,
                 qa_ref, ka_ref, va_ref, ob_ref, qc_ref, kc_ref, vc_ref, y_new_ref, y_ref):
    shift = mod_ref[0, :, 0:D_MODEL]
    scale = mod_ref[0, :, D_MODEL:2 * D_MODEL]
    h = (_layer_norm(x_tile()) * (1.0 + scale) + shift).astype(BF16)
    y_new_ref[...] = _dot(h, w_ref[...])

    def proj(off, width):
        return y_ref[:, off:off + width]

    qa_ref[...] = (proj(OFF_QA, D_A) * ATTN_SCALE).astype(BF16)
    ka_ref[...] = proj(OFF_KA, D_A).astype(BF16)
    va_ref[...] = proj(OFF_VA, D_A).astype(BF16)

    u = _gelu_tanh(proj(OFF_ZB, D_B))
    v = (_layer_norm(_gelu_tanh(proj(OFF_ZB + D_B, D_B))) * gsgu_ref[...]).astype(BF16)
    lane_group = lax.broadcasted_iota(jnp.int32, (SG_CHUNK, D_B), 1) // HEAD_DIM
    for c in range(TM // SG_CHUNK):
        rows = slice(c * SG_CHUNK, (c + 1) * SG_CHUNK)
        vch = v[rows]
        rhs = jnp.concatenate(
            [jnp.where(lane_group == g, vch, jnp.zeros_like(vch)) for g in range(SG_GROUPS)], 0)
        mixed = _dot(wsc_ref[...], rhs) + bsg_ref[...]
        ob_ref[rows, :] = (_rms(u[rows] * mixed) * gob_ref[...]).astype(BF16)

    cos_t = cos_ref[...]
    sin_t = sin_ref[...]
    for p in range(D_C // LANES):
        xq = proj(OFF_QC + p * LANES, LANES)
        xq = _rope(_pair_rms(xq, gq_ref[...]), cos_t, sin_t)
        qc_ref[:, p * LANES:(p + 1) * LANES] = (xq * (ATTN_SCALE * LOG2_E)).astype(BF16)
    xk = _rope(_pair_rms(proj(OFF_KC, D_KV_C), gk_ref[...]), cos_t, sin_t)
    kc_ref[...] = xk.astype(BF16)
    vc_ref[...] = proj(OFF_VC, D_KV_C).astype(BF16)


def _layer_spec(layer, shape):
    zeros = (0,) * len(shape)
    return pl.BlockSpec((None,) + tuple(shape), lambda *_: (layer,) + zeros)


def _inproj(x_a, x_b, mod, w_in_b, cos_t, sin_t, wsc, bsg, gsgu, gq2, gk2, gob, *, layer, n_tiles, n_lat,
            lat_per_batch, n_batch, rope_lat_tiles):
    t_rows = n_tiles * TM
    n_a_tiles = min(x_a.shape[0] // TM, n_tiles)

    def in_tile(t):
        return jnp.minimum(t, n_tiles - 1)

    def row_map(t):
        return (jnp.maximum(t - 1, 0), 0)

    def mod_map(t):
        tile = jnp.minimum(t, n_tiles - 1)
        return (layer, jnp.where(tile < n_lat, tile // lat_per_batch, n_batch), 0, 0)

    def rope_map(t):
        tile = jnp.maximum(t - 1, 0)
        return (jnp.where(tile < n_lat, tile % rope_lat_tiles, rope_lat_tiles), 0)

    widths = (D_A, D_A, D_A, D_B, D_C, D_KV_C, D_KV_C)
    return pl.pallas_call(
        functools.partial(_inproj_kernel, n_a_tiles=n_a_tiles, n_tiles=n_tiles),
        grid=(n_tiles + 1,),
        in_specs=_stream_specs(in_tile, n_a_tiles, x_b.shape[0] // TM) + [
            pl.BlockSpec((None, 1, 1, N_MOD * D_MODEL), mod_map),
            _layer_spec(layer, (D_MODEL, D_IN)),
            pl.BlockSpec((TM, LANES), rope_map),
            pl.BlockSpec((TM, LANES), rope_map),
            _layer_spec(layer, (SG_CHUNK, SG_GROUPS * SG_CHUNK)),
            _layer_spec(layer, (SG_CHUNK, D_B)),
            _layer_spec(layer, (1, D_B)),
            _layer_spec(layer, (1, LANES)),
            _layer_spec(layer, (1, LANES)),
            _layer_spec(layer, (1, D_B)),
        ],
        out_specs=[pl.BlockSpec((TM, w), row_map) for w in widths],
        out_shape=[jax.ShapeDtypeStruct((t_rows, w), BF16) for w in widths],
        scratch_shapes=[pltpu.VMEM((TM, D_IN), F32), pltpu.VMEM((TM, D_IN), F32)],
        compiler_params=pltpu.CompilerParams(
            dimension_semantics=("arbitrary",), vmem_limit_bytes=VMEM_LIMIT),
        name="inproj",
    )(x_a, x_b, mod, w_in_b, cos_t, sin_t, wsc, bsg, gsgu, gq2, gk2, gob)


def _split_heads(qg):
    lo = _low_half(qg.shape)
    zero = jnp.zeros_like(qg)
    return jnp.concatenate([jnp.where(lo, qg, zero), jnp.where(lo, zero, qg)], 0)


def _join_heads(o, m):
    return jnp.where(_low_half((m, LANES)), o[0:m], o[m:2 * m])


def _write_merged(o_ref, rows, outs, gout_ref):
    width = len(outs) * LANES
    ss = outs[0] * outs[0]
    for o in outs[1:]:
        ss = ss + o * o
    r = lax.rsqrt(jnp.sum(ss, -1, keepdims=True) * (1.0 / width) + LN_EPS)
    for p, o in enumerate(outs):
        cols = slice(p * LANES, (p + 1) * LANES)
        o_ref[rows, cols] = (o * r * gout_ref[:, cols]).astype(BF16)


def _na_latent(q_ref, k_ref, v_ref, kx_ref, vx_ref, bias_ref, gout_ref, o_ref, t, grid_rows):
    span = NA_UNION_ROWS * GRID_W
    first = jnp.clip(t * NA_ROWS_PER_STEP - NA_WIN_R // 2, 0, grid_rows - NA_UNION_ROWS)
    start = pl.multiple_of(first * GRID_W, GRID_W)
    outs = []
    for p in range(D_A // LANES):
        cols = slice(p * LANES, (p + 1) * LANES)
        lhs = _split_heads(q_ref[:, cols])
        bias = jnp.concatenate([bias_ref[p, 0], bias_ref[p, 1]], 0)
        s_loc = _dot_nt(lhs, k_ref[pl.ds(start, span), cols]) + bias
        s_ctx = _dot_nt(lhs, kx_ref[:, cols])
        m = jnp.maximum(jnp.max(s_loc, -1, keepdims=True), jnp.max(s_ctx, -1, keepdims=True))
        e_loc = jnp.exp(s_loc - m)
        e_ctx = jnp.exp(s_ctx - m)
        l = jnp.sum(e_loc, -1, keepdims=True) + jnp.sum(e_ctx, -1, keepdims=True)
        o = _dot(e_loc.astype(BF16), v_ref[pl.ds(start, span), cols]) \
            + _dot(e_ctx.astype(BF16), vx_ref[:, cols])
        outs.append(_join_heads(o * (1.0 / l), TQ))
    _write_merged(o_ref, slice(0, TQ), outs, gout_ref)


def _na_context(q_ref, kx_ref, vx_ref, gout_ref, o_ref):
    outs = []
    for p in range(D_A // LANES):
        cols = slice(p * LANES, (p + 1) * LANES)
        lhs = _split_heads(q_ref[:, cols])
        s = _dot_nt(lhs, kx_ref[:, cols])
        e = jnp.exp(s - jnp.max(s, -1, keepdims=True))
        o = _dot(e.astype(BF16), vx_ref[:, cols]) * (1.0 / jnp.sum(e, -1, keepdims=True))
        outs.append(_join_heads(o, TQ))
    _write_merged(o_ref, slice(0, TQ), outs, gout_ref)


def _gqa_fixed_shift(lhs_ref, k_ref, v_ref, kx_ref, vx_ref, l_ref, acc_ref, score_bound, n_k_chunks):
    def chunk(k_c, v_c):
        p = jnp.exp2(_dot_nt(lhs_ref[...], k_c) - score_bound)
        l_part = p[:, 0:LANES]
        for i in range(1, k_c.shape[0] // LANES):
            l_part = l_part + p[:, i * LANES:(i + 1) * LANES]
        return l_part, _dot(p.astype(BF16), v_c)

    l0, a0 = chunk(kx_ref[...], vx_ref[...])
    l_ref[...] = l0
    acc_ref[...] = a0
    for c in range(n_k_chunks):
        l_c, a_c = chunk(k_ref[c * TK:(c + 1) * TK, :], v_ref[c * TK:(c + 1) * TK, :])
        l_ref[...] += l_c
        acc_ref[...] += a_c
    acc_ref[...] = acc_ref[...] * (1.0 / jnp.sum(l_ref[...], -1, keepdims=True))


def _gqa_running_max(lhs_ref, k_ref, v_ref, kx_ref, vx_ref, m_ref, l_ref, acc_ref, latent, n_k_chunks):
    m_rows = GQA_Q_HEADS * TQ
    s = _dot_nt(lhs_ref[...], kx_ref[...])
    m0 = jnp.max(s, -1, keepdims=True)
    e = jnp.exp2(s - m0)
    m_ref[...] = jnp.broadcast_to(m0, (m_rows, LANES))
    l_ref[...] = jnp.broadcast_to(jnp.sum(e, -1, keepdims=True), (m_rows, LANES))
    acc_ref[...] = _dot(e.astype(BF16), vx_ref[...])

    @pl.when(latent)
    def _latent_keys():
        def body(c, carry):
            start = pl.multiple_of(c * TK, TK)
            s = _dot_nt(lhs_ref[...], k_ref[pl.ds(start, TK), :])
            m_prev = m_ref[:, 0:1]
            m_new = jnp.maximum(m_prev, jnp.max(s, -1, keepdims=True))
            alpha = jnp.exp2(m_prev - m_new)
            e = jnp.exp2(s - m_new)
            l_new = alpha * l_ref[:, 0:1] + jnp.sum(e, -1, keepdims=True)
            acc_ref[...] = alpha * acc_ref[...] + _dot(e.astype(BF16), v_ref[pl.ds(start, TK), :])
            m_ref[...] = jnp.broadcast_to(m_new, (m_rows, LANES))
            l_ref[...] = jnp.broadcast_to(l_new, (m_rows, LANES))
            return carry

        lax.fori_loop(0, n_k_chunks, body, 0)

    acc_ref[...] = acc_ref[...] * (1.0 / l_ref[:, 0:1])


def _attn_kernel(qa_ref, ka_ref, va_ref, kax_ref, vax_ref, bias_ref, gouta_ref,
                 qc_ref, kc_ref, vc_ref, kcx_ref, vcx_ref, gq_ref, gk_ref, goutc_ref,
                 oa_ref, oc_ref, lhs_ref, m_ref, l_ref, acc_ref, *, grid_rows, n_blocks, n_k_chunks):
    t = pl.program_id(1)
    latent = t < n_blocks
    n_groups = D_C // LANES

    for j in range(GQA_KV_HEADS):
        for g in range(n_groups):
            qg = qc_ref[:, g * LANES:(g + 1) * LANES]
            lo = _low_half(qg.shape)
            keep = lo if j == 0 else jnp.logical_not(lo)
            lhs_ref[(j * n_groups + g) * TQ:(j * n_groups + g + 1) * TQ, :] = jnp.where(keep, qg, jnp.zeros_like(qg))

    score_bound = (HEAD_DIM * ATTN_SCALE * LOG2_E) * jnp.max(jnp.abs(gq_ref[...])) * jnp.max(jnp.abs(gk_ref[...]))
    bounded = score_bound <= SAFE_SCORE_BOUND
    main_path = jnp.logical_and(bounded, latent)

    @pl.when(main_path)
    def _latent_bounded():
        _gqa_fixed_shift(lhs_ref, kc_ref, vc_ref, kcx_ref, vcx_ref, l_ref, acc_ref, score_bound, n_k_chunks)
        _na_latent(qa_ref, ka_ref, va_ref, kax_ref, vax_ref, bias_ref, gouta_ref, oa_ref, t, grid_rows)

    @pl.when(jnp.logical_not(main_path))
    def _other():
        @pl.when(latent)
        def _():
            _na_latent(qa_ref, ka_ref, va_ref, kax_ref, vax_ref, bias_ref, gouta_ref, oa_ref, t, grid_rows)

        @pl.when(jnp.logical_not(latent))
        def _():
            _na_context(qa_ref, kax_ref, vax_ref, gouta_ref, oa_ref)

        @pl.when(bounded)
        def _():
            _gqa_fixed_shift(lhs_ref, kc_ref, vc_ref, kcx_ref, vcx_ref, l_ref, acc_ref, score_bound, 0)

        @pl.when(jnp.logical_not(bounded))
        def _():
            _gqa_running_max(lhs_ref, kc_ref, vc_ref, kcx_ref, vcx_ref, m_ref, l_ref, acc_ref, latent, n_k_chunks)

    lo = _low_half((TQ, LANES))
    outs = [jnp.where(lo, acc_ref[g * TQ:(g + 1) * TQ, :], acc_ref[(n_groups + g) * TQ:(n_groups + g + 1) * TQ, :])
            for g in range(n_groups)]
    _write_merged(oc_ref, slice(0, TQ), outs, goutc_ref)


def _attn(qa, ka, va, bias, gout_a, qc, kc, vc, gq2, gk2, gout_c, *, layer, n_batch, seq, ctx_len, with_ctx):
    grid_rows = seq // GRID_W
    n_blocks = seq // TQ
    assert grid_rows >= NA_UNION_ROWS + NA_ROWS_PER_STEP
    ctx_block0 = n_batch * seq // ctx_len
    q_ctx_block0 = n_batch * seq // TQ
    out_rows = qa.shape[0] if with_ctx else n_batch * seq
    m_rows = GQA_Q_HEADS * TQ

    def q_map(b, t):
        return (jnp.where(t < n_blocks, b * n_blocks + t, q_ctx_block0 + b), 0)

    def batch_map(b, t):
        return (b, 0)

    def ctx_map(b, t):
        return (ctx_block0 + b, 0)

    def bias_map(b, t):
        return (layer, 0, 0, (t > 0).astype(jnp.int32) + (t >= n_blocks - 1).astype(jnp.int32), 0, 0)

    single = pl.Buffered(1)
    return pl.pallas_call(
        functools.partial(_attn_kernel, grid_rows=grid_rows, n_blocks=n_blocks, n_k_chunks=seq // TK),
        grid=(n_batch, n_blocks + (1 if with_ctx else 0)),
        in_specs=[
            pl.BlockSpec((TQ, D_A), q_map),
            pl.BlockSpec((seq, D_A), batch_map, pipeline_mode=single),
            pl.BlockSpec((seq, D_A), batch_map, pipeline_mode=single),
            pl.BlockSpec((ctx_len, D_A), ctx_map),
            pl.BlockSpec((ctx_len, D_A), ctx_map),
            pl.BlockSpec((None,) + bias.shape[1:3] + (None,) + bias.shape[4:], bias_map, pipeline_mode=single),
            _layer_spec(layer, (1, D_A)),
            pl.BlockSpec((TQ, D_C), q_map),
            pl.BlockSpec((seq, D_KV_C), batch_map),
            pl.BlockSpec((seq, D_KV_C), batch_map),
            pl.BlockSpec((ctx_len, D_KV_C), ctx_map),
            pl.BlockSpec((ctx_len, D_KV_C), ctx_map),
            _layer_spec(layer, (1, LANES)),
            _layer_spec(layer, (1, LANES)),
            _layer_spec(layer, (1, D_C)),
        ],
        out_specs=[pl.BlockSpec((TQ, D_A), q_map), pl.BlockSpec((TQ, D_C), q_map)],
        out_shape=[jax.ShapeDtypeStruct((out_rows, D_A), BF16), jax.ShapeDtypeStruct((out_rows, D_C), BF16)],
        scratch_shapes=[
            pltpu.VMEM((m_rows, LANES), BF16),
            pltpu.VMEM((m_rows, LANES), F32),
            pltpu.VMEM((m_rows, LANES), F32),
            pltpu.VMEM((m_rows, LANES), F32),
        ],
        compiler_params=pltpu.CompilerParams(
            dimension_semantics=("arbitrary", "arbitrary"), vmem_limit_bytes=VMEM_LIMIT),
        name="attn",
    )(qa, ka, va, ka, va, bias, gout_a, qc, kc, vc, kc, vc, gq2, gk2, gout_c)


def _post_kernel(oa_ref, ob_ref, oc_ref, xa_ref, xb_ref, mod_ref, wo_ref, wffn_ref, wout_ref,
                 ln1g_ref, ln1b_ref, ln2g_ref, ln2b_ref, o_ref, h_ref, g_ref, *, n_a_tiles, n_tiles):
    def mod(i):
        return mod_ref[0, :, i * D_MODEL:(i + 1) * D_MODEL]

    x = _stream_tile(pl.program_id(0), xa_ref, xb_ref, n_a_tiles, n_tiles)
    y = (_dot(oa_ref[...], wo_ref[0:D_A, :]) + _dot(ob_ref[...], wo_ref[D_A:D_A + D_B, :])
         + _dot(oc_ref[...], wo_ref[D_A + D_B:, :]))
    x1 = _layer_norm(ALPHA * x + mod(2) * y) * ln1g_ref[...] + ln1b_ref[...]
    h_ref[...] = (_layer_norm(x1) * (1.0 + mod(4)) + mod(3)).astype(BF16)
    for c in range(D_FF // FF_CHUNK):
        cols = slice(c * FF_CHUNK, (c + 1) * FF_CHUNK)
        gate = _dot(h_ref[...], wffn_ref[:, cols])
        up = _dot(h_ref[...], wffn_ref[:, D_FF + c * FF_CHUNK:D_FF + (c + 1) * FF_CHUNK])
        g_ref[:, cols] = (_silu(gate) * up).astype(BF16)
    ff = _dot(g_ref[...], wout_ref[...])
    o_ref[...] = _layer_norm(ALPHA * x1 + mod(5) * ff) * ln2g_ref[...] + ln2b_ref[...]


def _post(oa, ob, oc, x_a, x_b, mod, wo, wffn, wout, ln1g, ln1b, ln2g, ln2b, *, layer, n_lat, lat_per_batch,
          n_batch, n_tiles):
    n_a_tiles = min(x_a.shape[0] // TM, n_tiles)

    def row_map(t):
        return (t, 0)

    def mod_map(t):
        return (layer, jnp.where(t < n_lat, t // lat_per_batch, n_batch), 0, 0)

    def resident(shape):
        return pl.BlockSpec((None,) + shape, lambda t: (layer, 0, 0), pipeline_mode=pl.Buffered(1))

    return pl.pallas_call(
        functools.partial(_post_kernel, n_a_tiles=n_a_tiles, n_tiles=n_tiles),
        grid=(n_tiles,),
        in_specs=[
            pl.BlockSpec((TM, D_A), row_map),
            pl.BlockSpec((TM, D_B), row_map),
            pl.BlockSpec((TM, D_C), row_map),
        ] + _stream_specs(lambda t: t, n_a_tiles, x_b.shape[0] // TM) + [
            pl.BlockSpec((None, 1, 1, N_MOD * D_MODEL), mod_map),
            resident((D_MODEL, D_MODEL)),
            resident((D_MODEL, 2 * D_FF)),
            resident((D_FF, D_MODEL)),
            _layer_spec(layer, (1, D_MODEL)),
            _layer_spec(layer, (1, D_MODEL)),
            _layer_spec(layer, (1, D_MODEL)),
            _layer_spec(layer, (1, D_MODEL)),
        ],
        out_specs=pl.BlockSpec((TM, D_MODEL), row_map),
        out_shape=jax.ShapeDtypeStruct((n_tiles * TM, D_MODEL), F32),
        scratch_shapes=[pltpu.VMEM((TM, D_MODEL), BF16), pltpu.VMEM((TM, D_FF), BF16)],
        compiler_params=pltpu.CompilerParams(
            dimension_semantics=("arbitrary",), vmem_limit_bytes=VMEM_LIMIT),
        name="post",
    )(oa, ob, oc, x_a, x_b, mod, wo, wffn, wout, ln1g, ln1b, ln2g, ln2b)


def _rope_tables(seq):
    t = jnp.arange(seq, dtype=jnp.int32)
    row = (t // GRID_W).astype(F32)
    col = (t % GRID_W).astype(F32)
    n_freq = HEAD_DIM // 4
    inv = 1.0 / (ROPE_THETA ** (jnp.arange(n_freq, dtype=F32) / n_freq))
    ang = jnp.stack([row[:, None] * inv, col[:, None] * inv], axis=1)
    cos = jnp.cos(ang)[:, :, None, :]
    sin = jnp.sin(ang)[:, :, None, :]
    cos_h = jnp.broadcast_to(cos, (seq, 2, 2, n_freq)).reshape(seq, HEAD_DIM)
    sign = jnp.array([-1.0, 1.0], F32)[None, None, :, None]
    sin_h = jnp.broadcast_to(sin * sign, (seq, 2, 2, n_freq)).reshape(seq, HEAD_DIM)
    cos_t = jnp.concatenate([jnp.tile(cos_h, (1, 2)), jnp.ones((TM, LANES), F32)], 0)
    sin_t = jnp.concatenate([jnp.tile(sin_h, (1, 2)), jnp.zeros((TM, LANES), F32)], 0)
    return cos_t, sin_t


def _na_bias_tables(rpb):
    depth = rpb.shape[0]
    c_idx = np.arange(GRID_W)
    col_start = np.clip(c_idx - NA_WIN_C // 2, 0, GRID_W - NA_WIN_C)
    kc = np.arange(GRID_W)
    col_ok = (kc[None, :] >= col_start[:, None]) & (kc[None, :] < col_start[:, None] + NA_WIN_C)
    pad = GRID_W - NA_WIN_C
    padded = jnp.pad(rpb, ((0, 0), (0, 0), (0, 0), (pad, pad)))
    toep = jnp.stack([padded[..., GRID_W - 1 - c:2 * GRID_W - 1 - c] for c in range(GRID_W)], axis=-2)
    toep = jnp.where(jnp.asarray(col_ok), toep, MASK_VALUE)
    lead = np.array([0, NA_WIN_R // 2, NA_UNION_ROWS - NA_ROWS_PER_STEP])[:, None, None]
    i = np.arange(NA_ROWS_PER_STEP)[None, :, None]
    j = np.arange(NA_UNION_ROWS)[None, None, :]
    win_first = np.stack([np.zeros((NA_ROWS_PER_STEP, 1), np.int64), np.arange(NA_ROWS_PER_STEP)[:, None],
                          np.full((NA_ROWS_PER_STEP, 1), NA_UNION_ROWS - NA_WIN_R)])
    row_ok = (j >= win_first) & (j < win_first + NA_WIN_R)
    dr = j - lead - i + (NA_WIN_R - 1)
    masked = jnp.full((depth, NA_HEADS, GRID_W, GRID_W), MASK_VALUE, F32)
    per_row = []
    for v in range(3):
        for qi in range(NA_ROWS_PER_STEP):
            blocks = [toep[:, :, int(dr[v, qi, kj])] if row_ok[v, qi, kj] else masked
                      for kj in range(NA_UNION_ROWS)]
            per_row.append(jnp.concatenate(blocks, axis=-1))
    tab = jnp.concatenate(per_row, axis=-2)
    return tab.reshape(depth, NA_HEADS // 2, 2, 3, TQ, NA_UNION_ROWS * GRID_W)


def kernel(x, c, ctx, c_ctx, w_mod, b_mod, w_in, rpb, w_s, b_s, g_sgu, g_q, g_k, g_out, w_o,
           ln1_g, ln1_b, w_ffn_in, w_ffn_out, ln2_g, ln2_b):
    n_batch, seq, _ = x.shape
    ctx_len = ctx.shape[1]
    depth = w_in.shape[0]
    assert seq % TM == 0 and (n_batch * ctx_len) % TM == 0 and ctx_len == TQ and seq % TK == 0
    assert n_batch + 1 <= 8
    n_lat = n_batch * seq // TM
    n_tiles = n_lat + n_batch * ctx_len // TM
    lat_per_batch = seq // TM

    stream = (x.reshape(n_batch * seq, D_MODEL), ctx.reshape(n_batch * ctx_len, D_MODEL))

    c_all = jnp.zeros((8, D_MODEL), F32).at[:n_batch].set(c).at[n_batch].set(c_ctx)
    mod = _modulation(c_all, w_mod, b_mod)[:, :n_batch + 1].reshape(depth, n_batch + 1, 1, N_MOD * D_MODEL)

    def reorder_gqa_heads(a, axis, off):
        parts = [lax.slice_in_dim(a, 0, off, axis=axis)]
        parts += [lax.slice_in_dim(a, off + h * HEAD_DIM, off + (h + 1) * HEAD_DIM, axis=axis)
                  for h in GQA_HEAD_ORDER]
        parts.append(lax.slice_in_dim(a, off + D_C, a.shape[axis], axis=axis))
        return jnp.concatenate(parts, axis)

    def rows(v):
        return v.reshape(depth, 1, -1)

    w_in_b = reorder_gqa_heads(w_in, 2, OFF_QC).astype(BF16)
    w_o_b = reorder_gqa_heads(w_o, 1, D_A + D_B).astype(BF16)
    g_out_p = reorder_gqa_heads(g_out, 1, D_A + D_B)
    g_out_a, g_out_b, g_out_c = (rows(g_out_p[:, :D_A]), rows(g_out_p[:, D_A:D_A + D_B]),
                                 rows(g_out_p[:, D_A + D_B:]))
    w_ffn_b = w_ffn_in.astype(BF16)
    w_out_b = w_ffn_out.astype(BF16)
    w_s_cat = jnp.transpose(w_s, (0, 2, 1, 3)).reshape(depth, SG_CHUNK, SG_GROUPS * SG_CHUNK).astype(BF16)
    b_s_full = jnp.repeat(jnp.transpose(b_s, (0, 2, 1)), HEAD_DIM, axis=2)
    g_sgu3 = rows(g_sgu)
    g_q2 = rows(jnp.tile(g_q, (1, 2)))
    g_k2 = rows(jnp.tile(g_k, (1, 2)))
    ln1g, ln1b, ln2g, ln2b = rows(ln1_g), rows(ln1_b), rows(ln2_g), rows(ln2_b)
    cos_t, sin_t = _rope_tables(seq)
    bias = _na_bias_tables(rpb)

    for l in range(depth):
        with_ctx = l < depth - 1
        qa, ka, va, ob, qc, kc, vc = _inproj(
            *stream, mod, w_in_b, cos_t, sin_t, w_s_cat, b_s_full, g_sgu3, g_q2, g_k2, g_out_b,
            layer=l, n_tiles=n_tiles, n_lat=n_lat, lat_per_batch=lat_per_batch, n_batch=n_batch,
            rope_lat_tiles=seq // TM)
        oa, oc = _attn(qa, ka, va, bias, g_out_a, qc, kc, vc, g_q2, g_k2, g_out_c,
                       layer=l, n_batch=n_batch, seq=seq, ctx_len=ctx_len, with_ctx=with_ctx)
        x_all = _post(oa, ob, oc, *stream, mod, w_o_b, w_ffn_b, w_out_b, ln1g, ln1b, ln2g, ln2b,
                      layer=l, n_lat=n_lat, lat_per_batch=lat_per_batch, n_batch=n_batch,
                      n_tiles=n_tiles if with_ctx else n_lat)
        stream = (x_all, x_all)
    return x_all.reshape(n_batch, seq, D_MODEL)
```
